```python
import jax, jax.numpy as jnp
from jax import lax
import numpy as np

D_MODEL = 1024
BATCH = 8
SEQ = 2048
DEPTH = 1

CHUNK = 64
EPS = 1e-6
REC_WIDTH = D_MODEL // 2
REC_HEADS = 8
REC_HEAD_DIM = REC_WIDTH // REC_HEADS
CONV_WIDTH = 4
LRU_C = 8.0
ATT_HEAD_DIM = 64
ATT_HEADS = (D_MODEL // 2) // ATT_HEAD_DIM
ATT_WIDTH = ATT_HEADS * ATT_HEAD_DIM
LEFT_CHUNKS = 8
BAND = (LEFT_CHUNKS + 1) * CHUNK
MAX_REL = 128
MIX_WIDTH = REC_WIDTH + ATT_WIDTH
IN_WIDTH = 2 * REC_WIDTH + 3 * ATT_WIDTH
N_GROUPS = 4
EXPERTS_PER_GROUP = 4
N_EXPERTS = N_GROUPS * EXPERTS_PER_GROUP
TOP_K = 2
D_FF_EXPERT = D_MODEL // 2
DISPATCH_BLOCK = 128

kernel_name = 'hybrid_rglru_chunkattn_hiermoe'


def rms_norm(x, g):
    x32 = x.astype(jnp.float32)
    y = x32 * lax.rsqrt(jnp.mean(x32 * x32, axis=-1, keepdims=True) + EPS)
    return (y * g.astype(jnp.float32)).astype(x.dtype)


def rg_lru_group(u, gate, conv_w, conv_b, w_a, b_a, w_x, b_x, lam):
    B, S, _ = u.shape
    up = jnp.pad(u, ((0, 0), (CONV_WIDTH - 1, 0), (0, 0)))
    uc = conv_b
    for j in range(CONV_WIDTH):
        uc = uc + up[:, j:j + S] * conv_w[j]
    ub = uc.reshape(B, S, REC_HEADS, REC_HEAD_DIM)
    r = jax.nn.sigmoid(jnp.einsum('bshi,hij->bshj', ub, w_a) + b_a).reshape(B, S, REC_WIDTH)
    i = jax.nn.sigmoid(jnp.einsum('bshi,hij->bshj', ub, w_x) + b_x).reshape(B, S, REC_WIDTH)
    log_a = -LRU_C * r.astype(jnp.float32) * jax.nn.softplus(-lam.astype(jnp.float32))
    a = jnp.exp(log_a)
    bx = jnp.sqrt(-jnp.expm1(2.0 * log_a)) * (i * uc).astype(jnp.float32)

    def combine(left, right):
        a1, b1 = left
        a2, b2 = right
        return a1 * a2, a2 * b1 + b2

    _, h = lax.associative_scan(combine, (a, bx), axis=1)
    return h.astype(u.dtype) * jax.nn.gelu(gate)


def chunk_attention_group(q, k, v, rel_bias):
    B, S, _ = q.shape
    NC = S // CHUNK
    qc = q.reshape(B, NC, CHUNK, ATT_HEADS, ATT_HEAD_DIM) * (ATT_HEAD_DIM ** -0.5)
    pad = LEFT_CHUNKS * CHUNK

    def band(t):
        tc = jnp.pad(t.reshape(B, S, ATT_HEADS, ATT_HEAD_DIM), ((0, 0), (pad, 0), (0, 0), (0, 0)))
        tc = tc.reshape(B, NC + LEFT_CHUNKS, CHUNK, ATT_HEADS, ATT_HEAD_DIM)
        return jnp.concatenate([tc[:, j:j + NC] for j in range(LEFT_CHUNKS + 1)], axis=2)

    kb = band(k)
    vb = band(v)
    q_off = jnp.arange(CHUNK)
    k_off = jnp.arange(BAND)
    rel = jnp.clip(pad + q_off[:, None] - k_off[None, :], -MAX_REL, MAX_REL) + MAX_REL
    bias = rel_bias.astype(jnp.float32)[:, rel]
    key_pos = (jnp.arange(NC)[:, None] - LEFT_CHUNKS) * CHUNK + k_off[None, :]
    mask = jnp.where(key_pos >= 0, 0.0, -1e30).astype(jnp.float32)
    s = jnp.einsum('bcqhd,bckhd->bchqk', qc, kb, preferred_element_type=jnp.float32)
    s = s + bias[None, None] + mask[None, :, None, None, :]
    p = jax.nn.softmax(s, axis=-1).astype(v.dtype)
    o = jnp.einsum('bchqk,bckhd->bcqhd', p, vb)
    return o.reshape(B, S, ATT_WIDTH)


def hierarchical_moe(xn, w_group, b_group, w_router, b_router, w_e_gate, w_e_up, w_e_down):
    B, S, D = xn.shape
    T = B * S
    xt = xn.reshape(T, D)
    x32 = xt.astype(jnp.float32)
    group_logits = x32 @ w_group.astype(jnp.float32) + b_group.astype(jnp.float32)
    group_probs = jax.nn.softmax(group_logits, axis=-1)
    g = jnp.argmax(group_logits, axis=-1)
    p_g = jnp.take_along_axis(group_probs, g[:, None], axis=1)
    exp_logits = (x32 @ w_router.astype(jnp.float32) + b_router.astype(jnp.float32))
    exp_logits = exp_logits.reshape(T, N_GROUPS, EXPERTS_PER_GROUP)
    sel = jnp.take_along_axis(exp_logits, g[:, None, None], axis=1)[:, 0]
    top_v, top_i = lax.top_k(sel, TOP_K)
    weights = p_g * jax.nn.softmax(top_v, axis=-1)
    expert_id = g[:, None] * EXPERTS_PER_GROUP + top_i

    TK = T * TOP_K
    flat_e = expert_id.reshape(TK)
    flat_tok = jnp.repeat(jnp.arange(T), TOP_K)
    flat_w = weights.reshape(TK)
    order = jnp.argsort(flat_e)
    e_sorted = flat_e[order]
    tok_sorted = flat_tok[order]
    w_sorted = flat_w[order]
    counts = jnp.bincount(flat_e, length=N_EXPERTS)
    padded = (counts + DISPATCH_BLOCK - 1) // DISPATCH_BLOCK * DISPATCH_BLOCK
    start = jnp.cumsum(counts) - counts
    pend = jnp.cumsum(padded)
    pstart = pend - padded
    dest = pstart[e_sorted] + (jnp.arange(TK) - start[e_sorted])
    n_pad = ((TK + DISPATCH_BLOCK - 1) // DISPATCH_BLOCK + N_EXPERTS) * DISPATCH_BLOCK
    n_blocks = n_pad // DISPATCH_BLOCK
    x_buf = jnp.zeros((n_pad, D), xt.dtype).at[dest].set(xt[tok_sorted])
    block_e = jnp.clip(jnp.searchsorted(pend, jnp.arange(n_blocks) * DISPATCH_BLOCK, side='right'),
                       0, N_EXPERTS - 1)

    def expert_block(args):
        xb, e = args
        hid = jax.nn.silu(xb @ w_e_gate[e]) * (xb @ w_e_up[e])
        return hid @ w_e_down[e]

    y_buf = lax.map(expert_block, (x_buf.reshape(n_blocks, DISPATCH_BLOCK, D), block_e))
    y = y_buf.reshape(n_pad, D)[dest] * w_sorted[:, None].astype(xt.dtype)
    out = jax.ops.segment_sum(y, tok_sorted, num_segments=T)
    return out.reshape(B, S, D)


def setup_inputs(seed: int = 0) -> dict:
    key = jax.random.key(seed)
    ks = jax.random.split(key, 24)
    f32 = jnp.float32

    def nrm(k, shape, scale):
        return jax.random.normal(k, shape, f32) * scale

    a_base = jax.random.uniform(ks[9], (DEPTH, REC_WIDTH), f32, 0.9, 0.999)
    s = a_base ** (1.0 / LRU_C)
    lru_lambda = jnp.log(s) - jnp.log1p(-s)
    return {
        'x': nrm(ks[0], (BATCH, SEQ, D_MODEL), 1.0),
        'norm1_g': 1.0 + nrm(ks[1], (DEPTH, D_MODEL), 0.02),
        'w_in': nrm(ks[2], (DEPTH, D_MODEL, IN_WIDTH), D_MODEL ** -0.5),
        'conv_w': nrm(ks[3], (DEPTH, CONV_WIDTH, REC_WIDTH), CONV_WIDTH ** -0.5),
        'conv_b': nrm(ks[4], (DEPTH, REC_WIDTH), 0.02),
        'w_rg_a': nrm(ks[5], (DEPTH, REC_HEADS, REC_HEAD_DIM, REC_HEAD_DIM), REC_HEAD_DIM ** -0.5),
        'b_rg_a': nrm(ks[6], (DEPTH, REC_HEADS, REC_HEAD_DIM), 0.02),
        'w_rg_x': nrm(ks[7], (DEPTH, REC_HEADS, REC_HEAD_DIM, REC_HEAD_DIM), REC_HEAD_DIM ** -0.5),
        'b_rg_x': nrm(ks[8], (DEPTH, REC_HEADS, REC_HEAD_DIM), 0.02),
        'lru_lambda': lru_lambda,
        'rel_bias': nrm(ks[10], (DEPTH, ATT_HEADS, 2 * MAX_REL + 1), 0.2),
        'g_rec_out': 1.0 + nrm(ks[11], (DEPTH, REC_WIDTH), 0.02),
        'g_att_out': 1.0 + nrm(ks[12], (DEPTH, ATT_WIDTH), 0.02),
        'w_out': nrm(ks[13], (DEPTH, MIX_WIDTH, D_MODEL), MIX_WIDTH ** -0.5),
        'norm2_g': 1.0 + nrm(ks[14], (DEPTH, D_MODEL), 0.02),
        'w_group': nrm(ks[15], (DEPTH, D_MODEL, N_GROUPS), D_MODEL ** -0.5),
        'b_group': nrm(ks[16], (DEPTH, N_GROUPS), 0.01),
        'w_router': nrm(ks[17], (DEPTH, D_MODEL, N_EXPERTS), D_MODEL ** -0.5),
        'b_router': nrm(ks[18], (DEPTH, N_EXPERTS), 0.01),
        'w_e_gate': nrm(ks[19], (DEPTH, N_EXPERTS, D_MODEL, D_FF_EXPERT), D_MODEL ** -0.5),
        'w_e_up': nrm(ks[20], (DEPTH, N_EXPERTS, D_MODEL, D_FF_EXPERT), D_MODEL ** -0.5),
        'w_e_down': nrm(ks[21], (DEPTH, N_EXPERTS, D_FF_EXPERT, D_MODEL), D_FF_EXPERT ** -0.5),
        'final_g': 1.0 + nrm(ks[22], (D_MODEL,), 0.02),
    }


def reference(x, norm1_g, w_in, conv_w, conv_b, w_rg_a, b_rg_a, w_rg_x, b_rg_x, lru_lambda,
              rel_bias, g_rec_out, g_att_out, w_out, norm2_g, w_group, b_group, w_router,
              b_router, w_e_gate, w_e_up, w_e_down, final_g):
    splits = [REC_WIDTH, 2 * REC_WIDTH, 2 * REC_WIDTH + ATT_WIDTH, 2 * REC_WIDTH + 2 * ATT_WIDTH]
    h = x
    for l in range(DEPTH):
        xn = rms_norm(h, norm1_g[l])
        z = xn @ w_in[l]
        u, gate, q, k, v = jnp.split(z, splits, axis=-1)
        y_rec = rg_lru_group(u, gate, conv_w[l], conv_b[l], w_rg_a[l], b_rg_a[l],
                             w_rg_x[l], b_rg_x[l], lru_lambda[l])
        y_att = chunk_attention_group(q, k, v, rel_bias[l])
        mix = jnp.concatenate([rms_norm(y_rec, g_rec_out[l]), rms_norm(y_att, g_att_out[l])], axis=-1)
        h = h + mix @ w_out[l]
        h = h + hierarchical_moe(rms_norm(h, norm2_g[l]), w_group[l], b_group[l], w_router[l],
                                 b_router[l], w_e_gate[l], w_e_up[l], w_e_down[l])
    return rms_norm(h, final_g)
```

```python
import functools

import jax
import jax.numpy as jnp
from jax import lax
from jax.experimental import pallas as pl
from jax.experimental.pallas import tpu as pltpu

F32 = jnp.float32
BF16 = jnp.bfloat16

D_MODEL = 1024
CHUNK = 64
EPS = 1e-6
REC_WIDTH = 512
REC_HEADS = 8
REC_HEAD_DIM = 64
CONV_WIDTH = 4
LRU_C = 8.0
ATT_HEAD_DIM = 64
ATT_HEADS = 8
ATT_WIDTH = 512
LEFT_CHUNKS = 8
BAND = (LEFT_CHUNKS + 1) * CHUNK
MAX_REL = 128
IN_WIDTH = 2 * REC_WIDTH + 3 * ATT_WIDTH
N_GROUPS = 4
EXPERTS_PER_GROUP = 4
D_FF_EXPERT = 512

SUBLANES = 8
LANES = 128
VMEM_LIMIT_BYTES = 56 * 1024 * 1024

ROW_TILE = 512
SCAN_TILE = 256
EXPERT_BLOCK = 256
FINAL_TILE = 256
HEAD_PAIRS = ATT_HEADS // 2
PAD_ROWS = LEFT_CHUNKS * CHUNK
ROUTE_LANES = LANES
ROUTE_ROWS = SUBLANES


def _params(semantics):
    return pltpu.CompilerParams(dimension_semantics=semantics,
                                vmem_limit_bytes=VMEM_LIMIT_BYTES)


def _rms(x, g):
    ms = jnp.mean(x * x, axis=-1, keepdims=True)
    return (x * lax.rsqrt(ms + EPS)) * g


def _inproj_kernel(x_ref, g_ref, w_ref, u_ref, gate_ref, q_ref, k_ref, v_ref):
    xn = _rms(x_ref[...], g_ref[...])
    z = jnp.dot(xn.astype(BF16), w_ref[...], preferred_element_type=F32)
    u_ref[...] = z[:, :REC_WIDTH]
    gate_ref[...] = z[:, REC_WIDTH:2 * REC_WIDTH]
    o = 2 * REC_WIDTH
    q_ref[...] = (z[:, o:o + ATT_WIDTH] * (ATT_HEAD_DIM ** -0.5)).astype(BF16)
    k_ref[...] = z[:, o + ATT_WIDTH:o + 2 * ATT_WIDTH].astype(BF16)
    v_ref[...] = z[:, o + 2 * ATT_WIDTH:o + 3 * ATT_WIDTH].astype(BF16)


def _inproj(x2, g, w_bf):
    T = x2.shape[0]
    row = lambda i: (i, 0)
    const = lambda i: (0, 0)
    half = pl.BlockSpec((ROW_TILE, REC_WIDTH), row)
    return pl.pallas_call(
        _inproj_kernel,
        grid=(T // ROW_TILE,),
        in_specs=[pl.BlockSpec((ROW_TILE, D_MODEL), row),
                  pl.BlockSpec((1, D_MODEL), const),
                  pl.BlockSpec((D_MODEL, IN_WIDTH), const)],
        out_specs=[half, half, half, half, half],
        out_shape=[jax.ShapeDtypeStruct((T, REC_WIDTH), F32),
                   jax.ShapeDtypeStruct((T, REC_WIDTH), F32),
                   jax.ShapeDtypeStruct((T, ATT_WIDTH), BF16),
                   jax.ShapeDtypeStruct((T, ATT_WIDTH), BF16),
                   jax.ShapeDtypeStruct((T, ATT_WIDTH), BF16)],
        compiler_params=_params(("arbitrary",)),
        name="inproj",
    )(x2, g, w_bf)


def _rglru_kernel(u_ref, gate_ref, convw_ref, convb_ref, wab_ref, bab_ref, lam_ref, gout_ref,
                  o_ref, ext_ref, a_ref, b_ref, carry_ref):
    n = SCAN_TILE

    @pl.when(pl.program_id(1) == 0)
    def _():
        ext_ref[0:SUBLANES, :] = jnp.zeros((SUBLANES, REC_WIDTH), F32)
        carry_ref[...] = jnp.zeros((SUBLANES, REC_WIDTH), F32)

    u = u_ref[...]
    ext_ref[SUBLANES:SUBLANES + n, :] = u
    uc = convb_ref[...]
    for j in range(CONV_WIDTH):
        shift = CONV_WIDTH - 1 - j
        uc = uc + ext_ref[pl.ds(SUBLANES - shift, n), :] * convw_ref[j:j + 1, :]
    ext_ref[0:SUBLANES, :] = u[n - SUBLANES:, :]

    ri = jnp.dot(uc.astype(BF16), wab_ref[...], preferred_element_type=F32) + bab_ref[...]
    r = jax.nn.sigmoid(ri[:, :REC_WIDTH])
    ig = jax.nn.sigmoid(ri[:, REC_WIDTH:])
    log_a = (-LRU_C * r) * jax.nn.softplus(-lam_ref[...])
    a = jnp.exp(log_a)
    bx = jnp.sqrt(-jnp.tanh(log_a) * (a * a + 1.0)) * (ig * uc)

    row8 = lax.broadcasted_iota(jnp.int32, (n, REC_WIDTH), 0) & (SUBLANES - 1)
    for s in (1, 2, 4):
        keep = row8 >= s
        a_prev = jnp.where(keep, pltpu.roll(a, s, 0), 1.0)
        b_prev = jnp.where(keep, pltpu.roll(bx, s, 0), 0.0)
        bx = a * b_prev + bx
        a = a * a_prev
    a_ref[...] = a
    b_ref[...] = bx

    def group(gi, carry):
        r0 = pl.multiple_of(gi * SUBLANES, SUBLANES)
        h = b_ref[pl.ds(r0, SUBLANES), :] + a_ref[pl.ds(r0, SUBLANES), :] * carry
        b_ref[pl.ds(r0, SUBLANES), :] = h
        return jnp.broadcast_to(h[SUBLANES - 1:SUBLANES, :], (SUBLANES, REC_WIDTH))

    carry_ref[...] = lax.fori_loop(0, n // SUBLANES, group, carry_ref[...], unroll=4)

    y = b_ref[...] * jax.nn.gelu(gate_ref[...])
    o_ref[...] = _rms(y, gout_ref[...]).astype(BF16)


def _rglru(u3, gate3, conv_w, conv_b, wab_bf, bab, lam, g_out):
    B, S, _ = u3.shape
    tile = pl.BlockSpec((None, SCAN_TILE, REC_WIDTH), lambda b, i: (b, i, 0))
    const = lambda b, i: (0, 0)
    vec = pl.BlockSpec((1, REC_WIDTH), const)
    return pl.pallas_call(
        _rglru_kernel,
        grid=(B, S // SCAN_TILE),
        in_specs=[tile, tile,
                  pl.BlockSpec((CONV_WIDTH, REC_WIDTH), const), vec,
                  pl.BlockSpec((REC_WIDTH, 2 * REC_WIDTH), const),
                  pl.BlockSpec((1, 2 * REC_WIDTH), const), vec, vec],
        out_specs=tile,
        out_shape=jax.ShapeDtypeStruct((B, S, REC_WIDTH), BF16),
        scratch_shapes=[pltpu.VMEM((SUBLANES + SCAN_TILE, REC_WIDTH), F32),
                        pltpu.VMEM((SCAN_TILE, REC_WIDTH), F32),
                        pltpu.VMEM((SCAN_TILE, REC_WIDTH), F32),
                        pltpu.VMEM((SUBLANES, REC_WIDTH), F32)],
        compiler_params=_params(("arbitrary", "arbitrary")),
        name="rglru",
    )(u3, gate3, conv_w, conv_b, wab_bf, bab, lam, g_out)


def _attn_kernel(q_ref, k_ref, v_ref, bias_ref, gout_ref, o_ref, kpad_ref, vpad_ref):
    S = q_ref.shape[0]
    zeros = jnp.zeros((PAD_ROWS, ATT_WIDTH), BF16)
    kpad_ref[0:PAD_ROWS, :] = zeros
    vpad_ref[0:PAD_ROWS, :] = zeros
    kpad_ref[PAD_ROWS:PAD_ROWS + S, :] = k_ref[...]
    vpad_ref[PAD_ROWS:PAD_ROWS + S, :] = v_ref[...]

    pair_w = 2 * ATT_HEAD_DIM
    lane = lax.broadcasted_iota(jnp.int32, (CHUNK, pair_w), 1)
    first_head = lane < ATT_HEAD_DIM
    key_off = lax.broadcasted_iota(jnp.int32, (2 * CHUNK, BAND), 1)

    def chunk(c, masked):
        r0 = pl.multiple_of(c * CHUNK, CHUNK)
        q_rows = q_ref[pl.ds(r0, CHUNK), :]
        k_win = kpad_ref[pl.ds(r0, BAND), :]
        v_win = vpad_ref[pl.ds(r0, BAND), :]
        if masked:
            mask = jnp.where(key_off >= (LEFT_CHUNKS - c) * CHUNK, 0.0, -1e30)
        outs = []
        for p in range(HEAD_PAIRS):
            sl = slice(p * pair_w, (p + 1) * pair_w)
            q2 = q_rows[:, sl]
            zero = jnp.zeros_like(q2)
            q_stack = jnp.concatenate([jnp.where(first_head, q2, zero),
                                       jnp.where(first_head, zero, q2)], axis=0)
            s = lax.dot_general(q_stack, k_win[:, sl], (((1,), (1,)), ((), ())),
                                preferred_element_type=F32)
            s = s + bias_ref[p]
            if masked:
                s = s + mask
            m = jnp.max(s, axis=-1, keepdims=True)
            e = jnp.exp(s - m)
            denom = jnp.sum(e, axis=-1, keepdims=True)
            o = jnp.dot(e.astype(BF16), v_win[:, sl], preferred_element_type=F32)
            o = o / denom
            outs.append(jnp.where(first_head, o[:CHUNK], o[CHUNK:]))
        y = jnp.concatenate(outs, axis=1)
        o_ref[pl.ds(r0, CHUNK), :] = _rms(y, gout_ref[...]).astype(BF16)

    def masked_body(c, carry):
        chunk(c, True)
        return carry

    def plain_body(c, carry):
        chunk(c, False)
        return carry

    lax.fori_loop(0, LEFT_CHUNKS, masked_body, 0)
    lax.fori_loop(LEFT_CHUNKS, S // CHUNK, plain_body, 0)


def _attention(q3, k3, v3, bias_pairs, g_out):
    B, S, _ = q3.shape
    seq = pl.BlockSpec((None, S, ATT_WIDTH), lambda b: (b, 0, 0))
    return pl.pallas_call(
        _attn_kernel,
        grid=(B,),
        in_specs=[seq, seq, seq,
                  pl.BlockSpec((HEAD_PAIRS, 2 * CHUNK, BAND), lambda b: (0, 0, 0)),
                  pl.BlockSpec((1, ATT_WIDTH), lambda b: (0, 0))],
        out_specs=seq,
        out_shape=jax.ShapeDtypeStruct((B, S, ATT_WIDTH), BF16),
        scratch_shapes=[pltpu.VMEM((PAD_ROWS + S, ATT_WIDTH), BF16),
                        pltpu.VMEM((PAD_ROWS + S, ATT_WIDTH), BF16)],
        compiler_params=_params(("arbitrary",)),
        name="attn",
    )(q3, k3, v3, bias_pairs, g_out)


def _outproj_kernel(ma_ref, mb_ref, w_ref, x_ref, g2_ref, wrh_ref, wrl_ref, br_ref,
                    h_ref, xn_ref, route_ref):
    mix = jnp.concatenate([ma_ref[...], mb_ref[...]], axis=1)
    h = x_ref[...] + jnp.dot(mix, w_ref[...], preferred_element_type=F32)
    h_ref[...] = h
    xn = _rms(h, g2_ref[...])
    xn_ref[...] = xn

    x_hi = xn.astype(BF16)
    x_lo = (xn - x_hi.astype(F32)).astype(BF16)
    w_hi = wrh_ref[...]
    logits = (jnp.dot(x_hi, w_hi, preferred_element_type=F32)
              + jnp.dot(x_lo, w_hi, preferred_element_type=F32)
              + jnp.dot(x_hi, wrl_ref[...], preferred_element_type=F32)
              + br_ref[...])
    lt = logits.T
    row = lambda j: lt[j:j + 1, :]

    gl = [row(j) for j in range(N_GROUPS)]
    gmax = jnp.maximum(jnp.maximum(gl[0], gl[1]), jnp.maximum(gl[2], gl[3]))
    grp = jnp.where(gl[0] >= gmax, 0, jnp.where(gl[1] >= gmax, 1, jnp.where(gl[2] >= gmax, 2, 3)))
    denom = (jnp.exp(gl[0] - gmax) + jnp.exp(gl[1] - gmax)
             + jnp.exp(gl[2] - gmax) + jnp.exp(gl[3] - gmax))
    p_g = 1.0 / denom

    sel = []
    for j in range(EXPERTS_PER_GROUP):
        cand = [row(N_GROUPS + g * EXPERTS_PER_GROUP + j) for g in range(N_GROUPS)]
        sel.append(jnp.where(grp == 0, cand[0],
                             jnp.where(grp == 1, cand[1], jnp.where(grp == 2, cand[2], cand[3]))))

    def first_argmax(vals):
        top = jnp.maximum(jnp.maximum(vals[0], vals[1]), jnp.maximum(vals[2], vals[3]))
        idx = jnp.where(vals[0] >= top, 0,
                        jnp.where(vals[1] >= top, 1, jnp.where(vals[2] >= top, 2, 3)))
        return top, idx

    v1, i1 = first_argmax(sel)
    rest = [jnp.where(i1 == j, -jnp.inf, sel[j]) for j in range(EXPERTS_PER_GROUP)]
    v2, i2 = first_argmax(rest)
    t = jnp.exp(v2 - v1)
    w1 = p_g * (1.0 / (1.0 + t))
    w2 = p_g * (t / (1.0 + t))
    for j in range(EXPERTS_PER_GROUP):
        route_ref[j:j + 1, :] = jnp.where(i1 == j, w1, 0.0) + jnp.where(i2 == j, w2, 0.0)
    route_ref[EXPERTS_PER_GROUP:EXPERTS_PER_GROUP + 1, :] = grp.astype(F32)
    pad = ROUTE_ROWS - EXPERTS_PER_GROUP - 1
    route_ref[EXPERTS_PER_GROUP + 1:, :] = jnp.zeros((pad, route_ref.shape[1]), F32)


def _outproj(mix_a, mix_b, w_bf, x2, g2, wr_hi, wr_lo, br):
    T = x2.shape[0]
    row = lambda i: (i, 0)
    const = lambda i: (0, 0)
    full = pl.BlockSpec((ROW_TILE, D_MODEL), row)
    half = pl.BlockSpec((ROW_TILE, REC_WIDTH), row)
    return pl.pallas_call(
        _outproj_kernel,
        grid=(T // ROW_TILE,),
        in_specs=[half, half,
                  pl.BlockSpec((D_MODEL, D_MODEL), const),
                  full,
                  pl.BlockSpec((1, D_MODEL), const),
                  pl.BlockSpec((D_MODEL, ROUTE_LANES), const),
                  pl.BlockSpec((D_MODEL, ROUTE_LANES), const),
                  pl.BlockSpec((1, ROUTE_LANES), const)],
        out_specs=[full, full, pl.BlockSpec((ROUTE_ROWS, ROW_TILE), lambda i: (0, i))],
        out_shape=[jax.ShapeDtypeStruct((T, D_MODEL), F32),
                   jax.ShapeDtypeStruct((T, D_MODEL), F32),
                   jax.ShapeDtypeStruct((ROUTE_ROWS, T), F32)],
        compiler_params=_params(("arbitrary",)),
        name="outproj",
    )(mix_a, mix_b, w_bf, x2, g2, wr_hi, wr_lo, br)


def _gather_rows(idx_ref, base, n_rows, src_hbm, dst_ref, slot, sem):
    def body(r, carry):
        tok = idx_ref[base + r]
        pltpu.make_async_copy(src_hbm.at[pl.ds(tok, 1)], dst_ref.at[slot, pl.ds(r, 1)],
                              sem.at[slot]).start()
        return carry
    lax.fori_loop(0, n_rows, body, 0, unroll=8)


def _wait_rows(n_rows, src_hbm, dst_ref, slot, sem):
    def body(r, carry):
        pltpu.make_async_copy(src_hbm.at[pl.ds(0, 1)], dst_ref.at[slot, pl.ds(r, 1)],
                              sem.at[slot]).wait()
        return carry
    lax.fori_loop(0, n_rows, body, 0, unroll=8)


def _pipelined_gather(idx_ref, n_rows, src_hbm, buf_ref, sem):
    i = pl.program_id(0)
    slot = i % 2

    @pl.when(i == 0)
    def _():
        _gather_rows(idx_ref, 0, n_rows, src_hbm, buf_ref, 0, sem)

    @pl.when(i + 1 < pl.num_programs(0))
    def _():
        _gather_rows(idx_ref, (i + 1) * n_rows, n_rows, src_hbm, buf_ref, 1 - slot, sem)

    _wait_rows(n_rows, src_hbm, buf_ref, slot, sem)
    return slot


def _experts_kernel(src_ref, bgroup_ref, nvalid_ref, x_hbm, ws_ref, wg_ref, wu_ref, wd_ref,
                    y_ref, xbuf_ref, sem):
    del bgroup_ref
    slot = _pipelined_gather(src_ref, EXPERT_BLOCK, x_hbm, xbuf_ref, sem)
    active = pl.program_id(0) < nvalid_ref[0]

    @pl.when(active)
    def _():
        x = xbuf_ref[slot].astype(BF16)
        ws = ws_ref[...]
        hidden = []
        for j in range(EXPERTS_PER_GROUP):
            gate = jnp.dot(x, wg_ref[j], preferred_element_type=F32)
            up = jnp.dot(x, wu_ref[j], preferred_element_type=F32)
            hid = (jax.nn.silu(gate) * up) * ws[:, j:j + 1]
            hidden.append(hid.astype(BF16))
        hcat = jnp.concatenate(hidden, axis=1)
        wd = wd_ref[...].reshape(EXPERTS_PER_GROUP * D_FF_EXPERT, D_MODEL)
        y_ref[...] = jnp.dot(hcat, wd, preferred_element_type=F32)

    @pl.when(jnp.logical_not(active))
    def _():
        y_ref[...] = jnp.zeros(y_ref.shape, F32)


def _experts(src, bgroup, nvalid, xn, w_sorted, wg_bf, wu_bf, wd_bf):
    n_pad = src.shape[0]
    n_blocks = n_pad // EXPERT_BLOCK
    grp = lambda i, src, bg, nv: (bg[i], 0, 0)
    row = lambda i, src, bg, nv: (i, 0)
    grid_spec = pltpu.PrefetchScalarGridSpec(
        num_scalar_prefetch=3,
        grid=(n_blocks,),
        in_specs=[pl.BlockSpec(memory_space=pl.ANY),
                  pl.BlockSpec((EXPERT_BLOCK, EXPERTS_PER_GROUP), row),
                  pl.BlockSpec((EXPERTS_PER_GROUP, D_MODEL, D_FF_EXPERT), grp),
                  pl.BlockSpec((EXPERTS_PER_GROUP, D_MODEL, D_FF_EXPERT), grp),
                  pl.BlockSpec((EXPERTS_PER_GROUP, D_FF_EXPERT, D_MODEL), grp)],
        out_specs=pl.BlockSpec((EXPERT_BLOCK, D_MODEL), row),
        scratch_shapes=[pltpu.VMEM((2, EXPERT_BLOCK, D_MODEL), F32),
                        pltpu.SemaphoreType.DMA((2,))],
    )
    return pl.pallas_call(
        _experts_kernel,
        grid_spec=grid_spec,
        out_shape=jax.ShapeDtypeStruct((n_pad, D_MODEL), F32),
        compiler_params=_params(("arbitrary",)),
        name="experts",
    )(src, bgroup, nvalid, xn, w_sorted, wg_bf, wu_bf, wd_bf)


def _final_kernel(dest_ref, y_hbm, h_ref, g_ref, o_ref, ybuf_ref, sem):
    slot = _pipelined_gather(dest_ref, FINAL_TILE, y_hbm, ybuf_ref, sem)
    o_ref[...] = _rms(h_ref[...] + ybuf_ref[slot], g_ref[...])


def _final(dest, y_buf, h1, g):
    T = h1.shape[0]
    row = lambda i, dest: (i, 0)
    grid_spec = pltpu.PrefetchScalarGridSpec(
        num_scalar_prefetch=1,
        grid=(T // FINAL_TILE,),
        in_specs=[pl.BlockSpec(memory_space=pl.ANY),
                  pl.BlockSpec((FINAL_TILE, D_MODEL), row),
                  pl.BlockSpec((1, D_MODEL), lambda i, dest: (0, 0))],
        out_specs=pl.BlockSpec((FINAL_TILE, D_MODEL), row),
        scratch_shapes=[pltpu.VMEM((2, FINAL_TILE, D_MODEL), F32),
                        pltpu.SemaphoreType.DMA((2,))],
    )
    return pl.pallas_call(
        _final_kernel,
        grid_spec=grid_spec,
        out_shape=jax.ShapeDtypeStruct((T, D_MODEL), F32),
        compiler_params=_params(("arbitrary",)),
        name="final",
    )(dest, y_buf, h1, g)


def _block_diag(w):
    h, d, _ = w.shape
    eye = jnp.eye(h, dtype=w.dtype)
    return (eye[:, None, :, None] * w[:, :, None, :]).reshape(h * d, h * d)


def _bias_table(rel_bias):
    q_off = jnp.arange(CHUNK)
    k_off = jnp.arange(BAND)
    rel = jnp.clip(PAD_ROWS + q_off[:, None] - k_off[None, :], -MAX_REL, MAX_REL) + MAX_REL
    bias = rel_bias.astype(F32)[:, rel]
    return bias.reshape(HEAD_PAIRS, 2 * CHUNK, BAND)


def _dispatch_plan(grp, T):
    onehot = (grp[:, None] == jnp.arange(N_GROUPS)[None, :]).astype(jnp.int32)
    incl = jnp.cumsum(onehot, axis=0)
    counts = incl[-1]
    rank = jnp.sum((incl - onehot) * onehot, axis=1)
    padded = (counts + EXPERT_BLOCK - 1) // EXPERT_BLOCK * EXPERT_BLOCK
    pend = jnp.cumsum(padded)
    pstart = pend - padded
    dest = (pstart[grp] + rank).astype(jnp.int32)
    n_pad = T + N_GROUPS * EXPERT_BLOCK
    n_blocks = n_pad // EXPERT_BLOCK
    src = jnp.zeros((n_pad,), jnp.int32).at[dest].set(jnp.arange(T, dtype=jnp.int32))
    bstart = jnp.arange(n_blocks, dtype=jnp.int32) * EXPERT_BLOCK
    bgroup = jnp.sum((pend[None, :] <= bstart[:, None]).astype(jnp.int32), axis=1)
    bgroup = jnp.minimum(bgroup, N_GROUPS - 1)
    nvalid = (pend[-1:] // EXPERT_BLOCK).astype(jnp.int32)
    return dest, src, bgroup, nvalid


def kernel(x, norm1_g, w_in, conv_w, conv_b, w_rg_a, b_rg_a, w_rg_x, b_rg_x, lru_lambda,
           rel_bias, g_rec_out, g_att_out, w_out, norm2_g, w_group, b_group, w_router,
           b_router, w_e_gate, w_e_up, w_e_down, final_g):
    B, S, D = x.shape
    T = B * S
    assert w_in.shape[0] == 1, "single-layer block"
    l = 0
    h = x.reshape(T, D)
    u, gate, q, k, v = _inproj(h, norm1_g[l].reshape(1, D), w_in[l].astype(BF16))

    wab = jnp.concatenate([_block_diag(w_rg_a[l]), _block_diag(w_rg_x[l])], axis=1).astype(BF16)
    bab = jnp.concatenate([b_rg_a[l].reshape(1, REC_WIDTH), b_rg_x[l].reshape(1, REC_WIDTH)], axis=1)
    mix_a = _rglru(u.reshape(B, S, REC_WIDTH), gate.reshape(B, S, REC_WIDTH),
                   conv_w[l], conv_b[l].reshape(1, REC_WIDTH), wab, bab,
                   lru_lambda[l].reshape(1, REC_WIDTH), g_rec_out[l].reshape(1, REC_WIDTH))

    mix_b = _attention(q.reshape(B, S, ATT_WIDTH), k.reshape(B, S, ATT_WIDTH),
                       v.reshape(B, S, ATT_WIDTH), _bias_table(rel_bias[l]),
                       g_att_out[l].reshape(1, ATT_WIDTH))

    n_route = N_GROUPS + N_GROUPS * EXPERTS_PER_GROUP
    wr = jnp.concatenate([w_group[l].astype(F32), w_router[l].astype(F32)], axis=1)
    wr = jnp.pad(wr, ((0, 0), (0, ROUTE_LANES - n_route)))
    wr_hi = wr.astype(BF16)
    wr_lo = (wr - wr_hi.astype(F32)).astype(BF16)
    br = jnp.concatenate([b_group[l].astype(F32), b_router[l].astype(F32)])
    br = jnp.pad(br, (0, ROUTE_LANES - n_route)).reshape(1, ROUTE_LANES)
    h1, xn, route = _outproj(mix_a.reshape(T, REC_WIDTH), mix_b.reshape(T, ATT_WIDTH),
                             w_out[l].astype(BF16), h, norm2_g[l].reshape(1, D),
                             wr_hi, wr_lo, br)

    grp = route[EXPERTS_PER_GROUP].astype(jnp.int32)
    dest, src, bgroup, nvalid = _dispatch_plan(grp, T)
    w_tok = route[:EXPERTS_PER_GROUP].T
    w_sorted = jnp.zeros((src.shape[0], EXPERTS_PER_GROUP), F32).at[dest].set(w_tok)
    y_buf = _experts(src, bgroup, nvalid, xn, w_sorted, w_e_gate[l].astype(BF16),
                     w_e_up[l].astype(BF16), w_e_down[l].astype(BF16))
    out = _final(dest, y_buf, h1, final_g.reshape(1, D))
    return out.reshape(B, S, D)
```

```python
import functools

import jax
import jax.numpy as jnp
from jax import lax
from jax.experimental import pallas as pl
from jax.experimental.pallas import tpu as pltpu

F32 = jnp.float32
BF16 = jnp.bfloat16

D_MODEL = 1024
CHUNK = 64
EPS = 1e-6
REC_WIDTH = 512
REC_HEADS = 8
REC_HEAD_DIM = 64
CONV_WIDTH = 4
LRU_C = 8.0
ATT_HEAD_DIM = 64
ATT_HEADS = 8
ATT_WIDTH = 512
LEFT_CHUNKS = 8
BAND = (LEFT_CHUNKS + 1) * CHUNK
MAX_REL = 128
IN_WIDTH = 2 * REC_WIDTH + 3 * ATT_WIDTH
N_GROUPS = 4
EXPERTS_PER_GROUP = 4
D_FF_EXPERT = 512

SUBLANES = 8
LANES = 128
VMEM_LIMIT_BYTES = 56 * 1024 * 1024

ROW_TILE = 512
SCAN_TILE = 256
EXPERT_BLOCK = 256
FINAL_TILE = 256
Q_TILE = 2 * CHUNK
WINDOW = BAND + CHUNK
GROUP_HEADS = 4
GROUP_LANES = GROUP_HEADS * ATT_HEAD_DIM
HEAD_GROUPS = ATT_HEADS // GROUP_HEADS
PAD_ROWS = LEFT_CHUNKS * CHUNK
ROUTE_LANES = LANES
ROUTE_ROWS = SUBLANES


def _params(semantics):
    return pltpu.CompilerParams(dimension_semantics=semantics,
                                vmem_limit_bytes=VMEM_LIMIT_BYTES)


def _rms(x, g):
    ms = jnp.mean(x * x, axis=-1, keepdims=True)
    return (x * lax.rsqrt(ms + EPS)) * g


def _inproj_kernel(x_ref, g_ref, w_ref, u_ref, gate_ref, q_ref, k_ref, v_ref):
    xn = _rms(x_ref[...], g_ref[...])
    z = jnp.dot(xn.astype(BF16), w_ref[...], preferred_element_type=F32)
    u_ref[...] = z[:, :REC_WIDTH]
    gate_ref[...] = z[:, REC_WIDTH:2 * REC_WIDTH]
    o = 2 * REC_WIDTH
    q_ref[...] = (z[:, o:o + ATT_WIDTH] * (ATT_HEAD_DIM ** -0.5)).astype(BF16)
    k_ref[...] = z[:, o + ATT_WIDTH:o + 2 * ATT_WIDTH].astype(BF16)
    v_ref[...] = z[:, o + 2 * ATT_WIDTH:o + 3 * ATT_WIDTH].astype(BF16)


def _inproj(x2, g, w_bf):
    T = x2.shape[0]
    row = lambda i: (i, 0)
    const = lambda i: (0, 0)
    half = pl.BlockSpec((ROW_TILE, REC_WIDTH), row)
    return pl.pallas_call(
        _inproj_kernel,
        grid=(T // ROW_TILE,),
        in_specs=[pl.BlockSpec((ROW_TILE, D_MODEL), row),
                  pl.BlockSpec((1, D_MODEL), const),
                  pl.BlockSpec((D_MODEL, IN_WIDTH), const)],
        out_specs=[half, half, half, half, half],
        out_shape=[jax.ShapeDtypeStruct((T, REC_WIDTH), F32),
                   jax.ShapeDtypeStruct((T, REC_WIDTH), F32),
                   jax.ShapeDtypeStruct((T, ATT_WIDTH), BF16),
                   jax.ShapeDtypeStruct((T, ATT_WIDTH), BF16),
                   jax.ShapeDtypeStruct((T, ATT_WIDTH), BF16)],
        compiler_params=_params(("arbitrary",)),
        name="inproj",
    )(x2, g, w_bf)


def _rglru_kernel(u_ref, gate_ref, convw_ref, convb_ref, wab_ref, bab_ref, lam_ref, gout_ref,
                  o_ref, ext_ref, a_ref, b_ref, carry_ref):
    n = SCAN_TILE

    @pl.when(pl.program_id(1) == 0)
    def _():
        ext_ref[0:SUBLANES, :] = jnp.zeros((SUBLANES, REC_WIDTH), F32)
        carry_ref[...] = jnp.zeros((SUBLANES, REC_WIDTH), F32)

    u = u_ref[...]
    ext_ref[SUBLANES:SUBLANES + n, :] = u
    uc = convb_ref[...]
    for j in range(CONV_WIDTH):
        shift = CONV_WIDTH - 1 - j
        uc = uc + ext_ref[pl.ds(SUBLANES - shift, n), :] * convw_ref[j:j + 1, :]
    ext_ref[0:SUBLANES, :] = u[n - SUBLANES:, :]

    ri = jnp.dot(uc.astype(BF16), wab_ref[...], preferred_element_type=F32) + bab_ref[...]
    r = jax.nn.sigmoid(ri[:, :REC_WIDTH])
    ig = jax.nn.sigmoid(ri[:, REC_WIDTH:])
    log_a = (-LRU_C * r) * jax.nn.softplus(-lam_ref[...])
    a = jnp.exp(log_a)
    bx = jnp.sqrt(-jnp.tanh(log_a) * (a * a + 1.0)) * (ig * uc)

    row8 = lax.broadcasted_iota(jnp.int32, (n, REC_WIDTH), 0) & (SUBLANES - 1)
    for s in (1, 2, 4):
        keep = row8 >= s
        a_prev = jnp.where(keep, pltpu.roll(a, s, 0), 1.0)
        b_prev = jnp.where(keep, pltpu.roll(bx, s, 0), 0.0)
        bx = a * b_prev + bx
        a = a * a_prev
    a_ref[...] = a
    b_ref[...] = bx

    def group(gi, carry):
        r0 = pl.multiple_of(gi * SUBLANES, SUBLANES)
        h = b_ref[pl.ds(r0, SUBLANES), :] + a_ref[pl.ds(r0, SUBLANES), :] * carry
        b_ref[pl.ds(r0, SUBLANES), :] = h
        return jnp.broadcast_to(h[SUBLANES - 1:SUBLANES, :], (SUBLANES, REC_WIDTH))

    carry_ref[...] = lax.fori_loop(0, n // SUBLANES, group, carry_ref[...], unroll=4)

    y = b_ref[...] * jax.nn.gelu(gate_ref[...])
    o_ref[...] = _rms(y, gout_ref[...]).astype(BF16)


def _rglru(u3, gate3, conv_w, conv_b, wab_bf, bab, lam, g_out):
    B, S, _ = u3.shape
    tile = pl.BlockSpec((None, SCAN_TILE, REC_WIDTH), lambda b, i: (b, i, 0))
    const = lambda b, i: (0, 0)
    vec = pl.BlockSpec((1, REC_WIDTH), const)
    return pl.pallas_call(
        _rglru_kernel,
        grid=(B, S // SCAN_TILE),
        in_specs=[tile, tile,
                  pl.BlockSpec((CONV_WIDTH, REC_WIDTH), const), vec,
                  pl.BlockSpec((REC_WIDTH, 2 * REC_WIDTH), const),
                  pl.BlockSpec((1, 2 * REC_WIDTH), const), vec, vec],
        out_specs=tile,
        out_shape=jax.ShapeDtypeStruct((B, S, REC_WIDTH), BF16),
        scratch_shapes=[pltpu.VMEM((SUBLANES + SCAN_TILE, REC_WIDTH), F32),
                        pltpu.VMEM((SCAN_TILE, REC_WIDTH), F32),
                        pltpu.VMEM((SCAN_TILE, REC_WIDTH), F32),
                        pltpu.VMEM((SUBLANES, REC_WIDTH), F32)],
        compiler_params=_params(("arbitrary", "arbitrary")),
        name="rglru",
    )(u3, gate3, conv_w, conv_b, wab_bf, bab, lam, g_out)


def _attn_kernel(q_ref, k_ref, v_ref, bias_ref, gout_ref, o_ref, kpad_ref, vpad_ref):
    S = q_ref.shape[0]
    zeros = jnp.zeros((PAD_ROWS, ATT_WIDTH), BF16)
    kpad_ref[0:PAD_ROWS, :] = zeros
    vpad_ref[0:PAD_ROWS, :] = zeros
    kpad_ref[PAD_ROWS:PAD_ROWS + S, :] = k_ref[...]
    vpad_ref[PAD_ROWS:PAD_ROWS + S, :] = v_ref[...]

    lane_head = lax.broadcasted_iota(jnp.int32, (Q_TILE, GROUP_LANES), 1) // ATT_HEAD_DIM
    key_off = lax.broadcasted_iota(jnp.int32, (GROUP_HEADS * Q_TILE, WINDOW), 1)

    def tile(t, masked):
        r0 = pl.multiple_of(t * Q_TILE, Q_TILE)
        q_rows = q_ref[pl.ds(r0, Q_TILE), :]
        k_win = kpad_ref[pl.ds(r0, WINDOW), :]
        v_win = vpad_ref[pl.ds(r0, WINDOW), :]
        if masked:
            mask = jnp.where(key_off >= PAD_ROWS - t * Q_TILE, 0.0, -1e30)
        outs = []
        for hg in range(HEAD_GROUPS):
            sl = slice(hg * GROUP_LANES, (hg + 1) * GROUP_LANES)
            q4 = q_rows[:, sl]
            zero = jnp.zeros_like(q4)
            q_stack = jnp.concatenate(
                [jnp.where(lane_head == j, q4, zero) for j in range(GROUP_HEADS)], axis=0)
            s = lax.dot_general(q_stack, k_win[:, sl], (((1,), (1,)), ((), ())),
                                preferred_element_type=F32)
            s = s + bias_ref[hg]
            if masked:
                s = s + mask
            m = jnp.max(s, axis=-1, keepdims=True)
            e = jnp.exp(s - m)
            denom = jnp.sum(e, axis=-1, keepdims=True)
            o = jnp.dot(e.astype(BF16), v_win[:, sl], preferred_element_type=F32)
            o = o / denom
            out = o[:Q_TILE]
            for j in range(1, GROUP_HEADS):
                out = jnp.where(lane_head == j, o[j * Q_TILE:(j + 1) * Q_TILE], out)
            outs.append(out)
        y = jnp.concatenate(outs, axis=1)
        o_ref[pl.ds(r0, Q_TILE), :] = _rms(y, gout_ref[...]).astype(BF16)

    def masked_body(t, carry):
        tile(t, True)
        return carry

    def plain_body(t, carry):
        tile(t, False)
        return carry

    lax.fori_loop(0, PAD_ROWS // Q_TILE, masked_body, 0)
    lax.fori_loop(PAD_ROWS // Q_TILE, S // Q_TILE, plain_body, 0)


def _attention(q3, k3, v3, bias_tiles, g_out):
    B, S, _ = q3.shape
    seq = pl.BlockSpec((None, S, ATT_WIDTH), lambda b: (b, 0, 0))
    return pl.pallas_call(
        _attn_kernel,
        grid=(B,),
        in_specs=[seq, seq, seq,
                  pl.BlockSpec((HEAD_GROUPS, GROUP_HEADS * Q_TILE, WINDOW), lambda b: (0, 0, 0)),
                  pl.BlockSpec((1, ATT_WIDTH), lambda b: (0, 0))],
        out_specs=seq,
        out_shape=jax.ShapeDtypeStruct((B, S, ATT_WIDTH), BF16),
        scratch_shapes=[pltpu.VMEM((PAD_ROWS + S, ATT_WIDTH), BF16),
                        pltpu.VMEM((PAD_ROWS + S, ATT_WIDTH), BF16)],
        compiler_params=_params(("arbitrary",)),
        name="attn",
    )(q3, k3, v3, bias_tiles, g_out)


def _outproj_kernel(ma_ref, mb_ref, w_ref, x_ref, g2_ref, wrh_ref, wrl_ref, br_ref,
                    h_ref, xn_ref, route_ref):
    mix = jnp.concatenate([ma_ref[...], mb_ref[...]], axis=1)
    h = x_ref[...] + jnp.dot(mix, w_ref[...], preferred_element_type=F32)
    h_ref[...] = h
    xn = _rms(h, g2_ref[...])
    xn_ref[...] = xn

    x_hi = xn.astype(BF16)
    x_lo = (xn - x_hi.astype(F32)).astype(BF16)
    w_hi = wrh_ref[...]
    logits = (jnp.dot(x_hi, w_hi, preferred_element_type=F32)
              + jnp.dot(x_lo, w_hi, preferred_element_type=F32)
              + jnp.dot(x_hi, wrl_ref[...], preferred_element_type=F32)
              + br_ref[...])
    lt = logits.T
    row = lambda j: lt[j:j + 1, :]

    gl = [row(j) for j in range(N_GROUPS)]
    gmax = jnp.maximum(jnp.maximum(gl[0], gl[1]), jnp.maximum(gl[2], gl[3]))
    grp = jnp.where(gl[0] >= gmax, 0, jnp.where(gl[1] >= gmax, 1, jnp.where(gl[2] >= gmax, 2, 3)))
    denom = (jnp.exp(gl[0] - gmax) + jnp.exp(gl[1] - gmax)
             + jnp.exp(gl[2] - gmax) + jnp.exp(gl[3] - gmax))
    p_g = 1.0 / denom

    sel = []
    for j in range(EXPERTS_PER_GROUP):
        cand = [row(N_GROUPS + g * EXPERTS_PER_GROUP + j) for g in range(N_GROUPS)]
        sel.append(jnp.where(grp == 0, cand[0],
                             jnp.where(grp == 1, cand[1], jnp.where(grp == 2, cand[2], cand[3]))))

    def first_argmax(vals):
        top = jnp.maximum(jnp.maximum(vals[0], vals[1]), jnp.maximum(vals[2], vals[3]))
        idx = jnp.where(vals[0] >= top, 0,
                        jnp.where(vals[1] >= top, 1, jnp.where(vals[2] >= top, 2, 3)))
        return top, idx

    v1, i1 = first_argmax(sel)
    rest = [jnp.where(i1 == j, -jnp.inf, sel[j]) for j in range(EXPERTS_PER_GROUP)]
    v2, i2 = first_argmax(rest)
    t = jnp.exp(v2 - v1)
    w1 = p_g * (1.0 / (1.0 + t))
    w2 = p_g * (t / (1.0 + t))
    for j in range(EXPERTS_PER_GROUP):
        route_ref[j:j + 1, :] = jnp.where(i1 == j, w1, 0.0) + jnp.where(i2 == j, w2, 0.0)
    route_ref[EXPERTS_PER_GROUP:EXPERTS_PER_GROUP + 1, :] = grp.astype(F32)
    pad = ROUTE_ROWS - EXPERTS_PER_GROUP - 1
    route_ref[EXPERTS_PER_GROUP + 1:, :] = jnp.zeros((pad, route_ref.shape[1]), F32)


def _outproj(mix_a, mix_b, w_bf, x2, g2, wr_hi, wr_lo, br):
    T = x2.shape[0]
    row = lambda i: (i, 0)
    const = lambda i: (0, 0)
    full = pl.BlockSpec((ROW_TILE, D_MODEL), row)
    half = pl.BlockSpec((ROW_TILE, REC_WIDTH), row)
    return pl.pallas_call(
        _outproj_kernel,
        grid=(T // ROW_TILE,),
        in_specs=[half, half,
                  pl.BlockSpec((D_MODEL, D_MODEL), const),
                  full,
                  pl.BlockSpec((1, D_MODEL), const),
                  pl.BlockSpec((D_MODEL, ROUTE_LANES), const),
                  pl.BlockSpec((D_MODEL, ROUTE_LANES), const),
                  pl.BlockSpec((1, ROUTE_LANES), const)],
        out_specs=[full, full, pl.BlockSpec((ROUTE_ROWS, ROW_TILE), lambda i: (0, i))],
        out_shape=[jax.ShapeDtypeStruct((T, D_MODEL), F32),
                   jax.ShapeDtypeStruct((T, D_MODEL), F32),
                   jax.ShapeDtypeStruct((ROUTE_ROWS, T), F32)],
        compiler_params=_params(("arbitrary",)),
        name="outproj",
    )(mix_a, mix_b, w_bf, x2, g2, wr_hi, wr_lo, br)


def _gather_rows(idx_ref, base, n_rows, src_hbm, dst_ref, slot, sem):
    def body(r, carry):
        tok = idx_ref[base + r]
        pltpu.make_async_copy(src_hbm.at[pl.ds(tok, 1)], dst_ref.at[slot, pl.ds(r, 1)],
                              sem.at[slot]).start()
        return carry
    lax.fori_loop(0, n_rows, body, 0, unroll=8)


def _wait_rows(n_rows, src_hbm, dst_ref, slot, sem):
    def body(r, carry):
        pltpu.make_async_copy(src_hbm.at[pl.ds(0, 1)], dst_ref.at[slot, pl.ds(r, 1)],
                              sem.at[slot]).wait()
        return carry
    lax.fori_loop(0, n_rows, body, 0, unroll=8)


def _pipelined_gather(idx_ref, n_rows, src_hbm, buf_ref, sem):
    i = pl.program_id(0)
    slot = i % 2

    @pl.when(i == 0)
    def _():
        _gather_rows(idx_ref, 0, n_rows, src_hbm, buf_ref, 0, sem)

    @pl.when(i + 1 < pl.num_programs(0))
    def _():
        _gather_rows(idx_ref, (i + 1) * n_rows, n_rows, src_hbm, buf_ref, 1 - slot, sem)

    _wait_rows(n_rows, src_hbm, buf_ref, slot, sem)
    return slot


def _experts_kernel(src_ref, bgroup_ref, nvalid_ref, x_hbm, ws_ref, wg_ref, wu_ref, wd_ref,
                    y_ref, xbuf_ref, sem):
    del bgroup_ref
    slot = _pipelined_gather(src_ref, EXPERT_BLOCK, x_hbm, xbuf_ref, sem)
    active = pl.program_id(0) < nvalid_ref[0]

    @pl.when(active)
    def _():
        x = xbuf_ref[slot].astype(BF16)
        ws = ws_ref[...]
        hidden = []
        for j in range(EXPERTS_PER_GROUP):
            gate = jnp.dot(x, wg_ref[j], preferred_element_type=F32)
            up = jnp.dot(x, wu_ref[j], preferred_element_type=F32)
            hid = (jax.nn.silu(gate) * up) * ws[:, j:j + 1]
            hidden.append(hid.astype(BF16))
        hcat = jnp.concatenate(hidden, axis=1)
        wd = wd_ref[...].reshape(EXPERTS_PER_GROUP * D_FF_EXPERT, D_MODEL)
        y_ref[...] = jnp.dot(hcat, wd, preferred_element_type=F32)

    @pl.when(jnp.logical_not(active))
    def _():
        y_ref[...] = jnp.zeros(y_ref.shape, F32)


def _experts(src, bgroup, nvalid, xn, w_sorted, wg_bf, wu_bf, wd_bf):
    n_pad = src.shape[0]
    n_blocks = n_pad // EXPERT_BLOCK
    grp = lambda i, src, bg, nv: (bg[i], 0, 0)
    row = lambda i, src, bg, nv: (i, 0)
    grid_spec = pltpu.PrefetchScalarGridSpec(
        num_scalar_prefetch=3,
        grid=(n_blocks,),
        in_specs=[pl.BlockSpec(memory_space=pl.ANY),
                  pl.BlockSpec((EXPERT_BLOCK, EXPERTS_PER_GROUP), row),
                  pl.BlockSpec((EXPERTS_PER_GROUP, D_MODEL, D_FF_EXPERT), grp),
                  pl.BlockSpec((EXPERTS_PER_GROUP, D_MODEL, D_FF_EXPERT), grp),
                  pl.BlockSpec((EXPERTS_PER_GROUP, D_FF_EXPERT, D_MODEL), grp)],
        out_specs=pl.BlockSpec((EXPERT_BLOCK, D_MODEL), row),
        scratch_shapes=[pltpu.VMEM((2, EXPERT_BLOCK, D_MODEL), F32),
                        pltpu.SemaphoreType.DMA((2,))],
    )
    return pl.pallas_call(
        _experts_kernel,
        grid_spec=grid_spec,
        out_shape=jax.ShapeDtypeStruct((n_pad, D_MODEL), F32),
        compiler_params=_params(("arbitrary",)),
        name="experts",
    )(src, bgroup, nvalid, xn, w_sorted, wg_bf, wu_bf, wd_bf)


def _final_kernel(dest_ref, y_hbm, h_ref, g_ref, o_ref, ybuf_ref, sem):
    slot = _pipelined_gather(dest_ref, FINAL_TILE, y_hbm, ybuf_ref, sem)
    o_ref[...] = _rms(h_ref[...] + ybuf_ref[slot], g_ref[...])


def _final(dest, y_buf, h1, g):
    T = h1.shape[0]
    row = lambda i, dest: (i, 0)
    grid_spec = pltpu.PrefetchScalarGridSpec(
        num_scalar_prefetch=1,
        grid=(T // FINAL_TILE,),
        in_specs=[pl.BlockSpec(memory_space=pl.ANY),
                  pl.BlockSpec((FINAL_TILE, D_MODEL), row),
                  pl.BlockSpec((1, D_MODEL), lambda i, dest: (0, 0))],
        out_specs=pl.BlockSpec((FINAL_TILE, D_MODEL), row),
        scratch_shapes=[pltpu.VMEM((2, FINAL_TILE, D_MODEL), F32),
                        pltpu.SemaphoreType.DMA((2,))],
    )
    return pl.pallas_call(
        _final_kernel,
        grid_spec=grid_spec,
        out_shape=jax.ShapeDtypeStruct((T, D_MODEL), F32),
        compiler_params=_params(("arbitrary",)),
        name="final",
    )(dest, y_buf, h1, g)


def _block_diag(w):
    h, d, _ = w.shape
    eye = jnp.eye(h, dtype=w.dtype)
    return (eye[:, None, :, None] * w[:, :, None, :]).reshape(h * d, h * d)


def _bias_table(rel_bias):
    rb = rel_bias.astype(F32)
    n_far = PAD_ROWS - MAX_REL + CHUNK
    far = jnp.broadcast_to(rb[:, 2 * MAX_REL:], (ATT_HEADS, n_far))
    near = rb[:, MAX_REL - CHUNK + 1:2 * MAX_REL][:, ::-1]
    diag = jnp.concatenate([far, near], axis=1)
    bias = jnp.stack([diag[:, CHUNK - 1 - q:CHUNK - 1 - q + BAND] for q in range(CHUNK)], axis=1)
    off = jnp.full((ATT_HEADS, CHUNK, CHUNK), -1e30, F32)
    first = jnp.concatenate([bias, off], axis=2)
    second = jnp.concatenate([off, bias], axis=2)
    tiles = jnp.concatenate([first, second], axis=1)
    return tiles.reshape(HEAD_GROUPS, GROUP_HEADS * Q_TILE, WINDOW)


def _dispatch_plan(grp, T):
    onehot = (grp[:, None] == jnp.arange(N_GROUPS)[None, :]).astype(jnp.int32)
    incl = jnp.cumsum(onehot, axis=0)
    counts = incl[-1]
    padded = (counts + EXPERT_BLOCK - 1) // EXPERT_BLOCK * EXPERT_BLOCK
    pend = jnp.cumsum(padded)
    pstart = pend - padded
    dest = jnp.sum((incl - onehot + pstart[None, :]) * onehot, axis=1).astype(jnp.int32)
    n_pad = T + N_GROUPS * EXPERT_BLOCK
    n_blocks = n_pad // EXPERT_BLOCK
    src = jnp.zeros((n_pad,), jnp.int32).at[dest].set(jnp.arange(T, dtype=jnp.int32))
    bstart = jnp.arange(n_blocks, dtype=jnp.int32) * EXPERT_BLOCK
    bgroup = jnp.sum((pend[None, :] <= bstart[:, None]).astype(jnp.int32), axis=1)
    bgroup = jnp.minimum(bgroup, N_GROUPS - 1)
    nvalid = (pend[-1:] // EXPERT_BLOCK).astype(jnp.int32)
    return dest, src, bgroup, nvalid


def kernel(x, norm1_g, w_in, conv_w, conv_b, w_rg_a, b_rg_a, w_rg_x, b_rg_x, lru_lambda,
           rel_bias, g_rec_out, g_att_out, w_out, norm2_g, w_group, b_group, w_router,
           b_router, w_e_gate, w_e_up, w_e_down, final_g):
    B, S, D = x.shape
    T = B * S
    assert w_in.shape[0] == 1, "single-layer block"
    l = 0
    h = x.reshape(T, D)
    u, gate, q, k, v = _inproj(h, norm1_g[l].reshape(1, D), w_in[l].astype(BF16))

    wab = jnp.concatenate([_block_diag(w_rg_a[l]), _block_diag(w_rg_x[l])], axis=1).astype(BF16)
    bab = jnp.concatenate([b_rg_a[l].reshape(1, REC_WIDTH), b_rg_x[l].reshape(1, REC_WIDTH)], axis=1)
    mix_a = _rglru(u.reshape(B, S, REC_WIDTH), gate.reshape(B, S, REC_WIDTH),
                   conv_w[l], conv_b[l].reshape(1, REC_WIDTH), wab, bab,
                   lru_lambda[l].reshape(1, REC_WIDTH), g_rec_out[l].reshape(1, REC_WIDTH))

    mix_b = _attention(q.reshape(B, S, ATT_WIDTH), k.reshape(B, S, ATT_WIDTH),
                       v.reshape(B, S, ATT_WIDTH), _bias_table(rel_bias[l]),
                       g_att_out[l].reshape(1, ATT_WIDTH))

    n_route = N_GROUPS + N_GROUPS * EXPERTS_PER_GROUP
    wr = jnp.concatenate([w_group[l].astype(F32), w_router[l].astype(F32)], axis=1)
    wr = jnp.pad(wr, ((0, 0), (0, ROUTE_LANES - n_route)))
    wr_hi = wr.astype(BF16)
    wr_lo = (wr - wr_hi.astype(F32)).astype(BF16)
    br = jnp.concatenate([b_group[l].astype(F32), b_router[l].astype(F32)])
    br = jnp.pad(br, (0, ROUTE_LANES - n_route)).reshape(1, ROUTE_LANES)
    h1, xn, route = _outproj(mix_a.reshape(T, REC_WIDTH), mix_b.reshape(T, ATT_WIDTH),
                             w_out[l].astype(BF16), h, norm2_g[l].reshape(1, D),
                             wr_hi, wr_lo, br)

    grp = route[EXPERTS_PER_GROUP].astype(jnp.int32)
    dest, src, bgroup, nvalid = _dispatch_plan(grp, T)
    w_tok = route[:EXPERTS_PER_GROUP].T
    w_sorted = jnp.zeros((src.shape[0], EXPERTS_PER_GROUP), F32).at[dest].set(w_tok)
    y_buf = _experts(src, bgroup, nvalid, xn, w_sorted, w_e_gate[l].astype(BF16),
                     w_e_up[l].astype(BF16), w_e_down[l].astype(BF16))
    out = _final(dest, y_buf, h1, final_g.reshape(1, D))
    return out.reshape(B, S, D)
```

```python
import functools

import jax
import jax.numpy as jnp
from jax import lax
from jax.experimental import pallas as pl
from jax.experimental.pallas import tpu as pltpu

F32 = jnp.float32
BF16 = jnp.bfloat16

D_MODEL = 1024
CHUNK = 64
EPS = 1e-6
REC_WIDTH = 512
REC_HEADS = 8
REC_HEAD_DIM = 64
CONV_WIDTH = 4
LRU_C = 8.0
ATT_HEAD_DIM = 64
ATT_HEADS = 8
ATT_WIDTH = 512
LEFT_CHUNKS = 8
BAND = (LEFT_CHUNKS + 1) * CHUNK
MAX_REL = 128
IN_WIDTH = 2 * REC_WIDTH + 3 * ATT_WIDTH
N_GROUPS = 4
EXPERTS_PER_GROUP = 4
D_FF_EXPERT = 512

SUBLANES = 8
LANES = 128
VMEM_LIMIT_BYTES = 56 * 1024 * 1024

ROW_TILE = 512
SCAN_TILE = 256
EXPERT_BLOCK = 256
FINAL_TILE = 256
Q_TILE = 2 * CHUNK
WINDOW = BAND + CHUNK
GROUP_HEADS = 4
GROUP_LANES = GROUP_HEADS * ATT_HEAD_DIM
HEAD_GROUPS = ATT_HEADS // GROUP_HEADS
PAD_ROWS = LEFT_CHUNKS * CHUNK
ROUTE_LANES = LANES
ROUTE_ROWS = SUBLANES
TOKEN_ROW = D_MODEL + ROUTE_LANES


def _params(semantics):
    return pltpu.CompilerParams(dimension_semantics=semantics,
                                vmem_limit_bytes=VMEM_LIMIT_BYTES)


def _rms(x, g):
    ms = jnp.mean(x * x, axis=-1, keepdims=True)
    return (x * lax.rsqrt(ms + EPS)) * g


def _inproj_kernel(x_ref, g_ref, w_ref, u_ref, gate_ref, q_ref, k_ref, v_ref):
    xn = _rms(x_ref[...], g_ref[...])
    z = jnp.dot(xn.astype(BF16), w_ref[...], preferred_element_type=F32)
    u_ref[...] = z[:, :REC_WIDTH]
    gate_ref[...] = z[:, REC_WIDTH:2 * REC_WIDTH]
    o = 2 * REC_WIDTH
    q_ref[...] = (z[:, o:o + ATT_WIDTH] * (ATT_HEAD_DIM ** -0.5)).astype(BF16)
    k_ref[...] = z[:, o + ATT_WIDTH:o + 2 * ATT_WIDTH].astype(BF16)
    v_ref[...] = z[:, o + 2 * ATT_WIDTH:o + 3 * ATT_WIDTH].astype(BF16)


def _inproj(x2, g, w_bf):
    T = x2.shape[0]
    row = lambda i: (i, 0)
    const = lambda i: (0, 0)
    half = pl.BlockSpec((ROW_TILE, REC_WIDTH), row)
    return pl.pallas_call(
        _inproj_kernel,
        grid=(T // ROW_TILE,),
        in_specs=[pl.BlockSpec((ROW_TILE, D_MODEL), row),
                  pl.BlockSpec((1, D_MODEL), const),
                  pl.BlockSpec((D_MODEL, IN_WIDTH), const)],
        out_specs=[half, half, half, half, half],
        out_shape=[jax.ShapeDtypeStruct((T, REC_WIDTH), F32),
                   jax.ShapeDtypeStruct((T, REC_WIDTH), F32),
                   jax.ShapeDtypeStruct((T, ATT_WIDTH), BF16),
                   jax.ShapeDtypeStruct((T, ATT_WIDTH), BF16),
                   jax.ShapeDtypeStruct((T, ATT_WIDTH), BF16)],
        compiler_params=_params(("arbitrary",)),
        name="inproj",
    )(x2, g, w_bf)


def _rglru_kernel(u_ref, gate_ref, convw_ref, convb_ref, wab_ref, bab_ref, lam_ref, gout_ref,
                  o_ref, ext_ref, a_ref, b_ref, carry_ref):
    n = SCAN_TILE

    @pl.when(pl.program_id(1) == 0)
    def _():
        ext_ref[0:SUBLANES, :] = jnp.zeros((SUBLANES, REC_WIDTH), F32)
        carry_ref[...] = jnp.zeros((SUBLANES, REC_WIDTH), F32)

    u = u_ref[...]
    ext_ref[SUBLANES:SUBLANES + n, :] = u
    uc = convb_ref[...]
    for j in range(CONV_WIDTH):
        shift = CONV_WIDTH - 1 - j
        uc = uc + ext_ref[pl.ds(SUBLANES - shift, n), :] * convw_ref[j:j + 1, :]
    ext_ref[0:SUBLANES, :] = u[n - SUBLANES:, :]

    ri = jnp.dot(uc.astype(BF16), wab_ref[...], preferred_element_type=F32) + bab_ref[...]
    r = jax.nn.sigmoid(ri[:, :REC_WIDTH])
    ig = jax.nn.sigmoid(ri[:, REC_WIDTH:])
    log_a = (-LRU_C * r) * jax.nn.softplus(-lam_ref[...])
    a = jnp.exp(log_a)
    bx = jnp.sqrt(-jnp.tanh(log_a) * (a * a + 1.0)) * (ig * uc)

    row8 = lax.broadcasted_iota(jnp.int32, (n, REC_WIDTH), 0) & (SUBLANES - 1)
    for s in (1, 2, 4):
        keep = row8 >= s
        a_prev = jnp.where(keep, pltpu.roll(a, s, 0), 1.0)
        b_prev = jnp.where(keep, pltpu.roll(bx, s, 0), 0.0)
        bx = a * b_prev + bx
        a = a * a_prev
    a_ref[...] = a
    b_ref[...] = bx

    def group(gi, carry):
        r0 = pl.multiple_of(gi * SUBLANES, SUBLANES)
        h = b_ref[pl.ds(r0, SUBLANES), :] + a_ref[pl.ds(r0, SUBLANES), :] * carry
        b_ref[pl.ds(r0, SUBLANES), :] = h
        return jnp.broadcast_to(h[SUBLANES - 1:SUBLANES, :], (SUBLANES, REC_WIDTH))

    carry_ref[...] = lax.fori_loop(0, n // SUBLANES, group, carry_ref[...], unroll=4)

    y = b_ref[...] * jax.nn.gelu(gate_ref[...])
    o_ref[...] = _rms(y, gout_ref[...]).astype(BF16)


def _rglru(u3, gate3, conv_w, conv_b, wab_bf, bab, lam, g_out):
    B, S, _ = u3.shape
    tile = pl.BlockSpec((None, SCAN_TILE, REC_WIDTH), lambda b, i: (b, i, 0))
    const = lambda b, i: (0, 0)
    vec = pl.BlockSpec((1, REC_WIDTH), const)
    return pl.pallas_call(
        _rglru_kernel,
        grid=(B, S // SCAN_TILE),
        in_specs=[tile, tile,
                  pl.BlockSpec((CONV_WIDTH, REC_WIDTH), const), vec,
                  pl.BlockSpec((REC_WIDTH, 2 * REC_WIDTH), const),
                  pl.BlockSpec((1, 2 * REC_WIDTH), const), vec, vec],
        out_specs=tile,
        out_shape=jax.ShapeDtypeStruct((B, S, REC_WIDTH), BF16),
        scratch_shapes=[pltpu.VMEM((SUBLANES + SCAN_TILE, REC_WIDTH), F32),
                        pltpu.VMEM((SCAN_TILE, REC_WIDTH), F32),
                        pltpu.VMEM((SCAN_TILE, REC_WIDTH), F32),
                        pltpu.VMEM((SUBLANES, REC_WIDTH), F32)],
        compiler_params=_params(("arbitrary", "arbitrary")),
        name="rglru",
    )(u3, gate3, conv_w, conv_b, wab_bf, bab, lam, g_out)


def _attn_kernel(q_ref, k_ref, v_ref, bias_ref, gout_ref, o_ref, kpad_ref, vpad_ref):
    S = q_ref.shape[0]
    zeros = jnp.zeros((PAD_ROWS, ATT_WIDTH), BF16)
    kpad_ref[0:PAD_ROWS, :] = zeros
    vpad_ref[0:PAD_ROWS, :] = zeros
    kpad_ref[PAD_ROWS:PAD_ROWS + S, :] = k_ref[...]
    vpad_ref[PAD_ROWS:PAD_ROWS + S, :] = v_ref[...]

    lane_head = lax.broadcasted_iota(jnp.int32, (Q_TILE, GROUP_LANES), 1) // ATT_HEAD_DIM
    key_off = lax.broadcasted_iota(jnp.int32, (GROUP_HEADS * Q_TILE, WINDOW), 1)

    def tile(t, masked):
        r0 = pl.multiple_of(t * Q_TILE, Q_TILE)
        q_rows = q_ref[pl.ds(r0, Q_TILE), :]
        k_win = kpad_ref[pl.ds(r0, WINDOW), :]
        v_win = vpad_ref[pl.ds(r0, WINDOW), :]
        if masked:
            mask = jnp.where(key_off >= PAD_ROWS - t * Q_TILE, 0.0, -1e30)
        outs = []
        for hg in range(HEAD_GROUPS):
            sl = slice(hg * GROUP_LANES, (hg + 1) * GROUP_LANES)
            q4 = q_rows[:, sl]
            zero = jnp.zeros_like(q4)
            q_stack = jnp.concatenate(
                [jnp.where(lane_head == j, q4, zero) for j in range(GROUP_HEADS)], axis=0)
            s = lax.dot_general(q_stack, k_win[:, sl], (((1,), (1,)), ((), ())),
                                preferred_element_type=F32)
            s = s + bias_ref[hg]
            if masked:
                s = s + mask
            m = jnp.max(s, axis=-1, keepdims=True)
            e = jnp.exp(s - m)
            denom = jnp.sum(e, axis=-1, keepdims=True)
            o = jnp.dot(e.astype(BF16), v_win[:, sl], preferred_element_type=F32)
            o = o / denom
            out = o[:Q_TILE]
            for j in range(1, GROUP_HEADS):
                out = jnp.where(lane_head == j, o[j * Q_TILE:(j + 1) * Q_TILE], out)
            outs.append(out)
        y = jnp.concatenate(outs, axis=1)
        o_ref[pl.ds(r0, Q_TILE), :] = _rms(y, gout_ref[...]).astype(BF16)

    def masked_body(t, carry):
        tile(t, True)
        return carry

    def plain_body(t, carry):
        tile(t, False)
        return carry

    lax.fori_loop(0, PAD_ROWS // Q_TILE, masked_body, 0)
    lax.fori_loop(PAD_ROWS // Q_TILE, S // Q_TILE, plain_body, 0)


def _attention(q3, k3, v3, bias_tiles, g_out):
    B, S, _ = q3.shape
    seq = pl.BlockSpec((None, S, ATT_WIDTH), lambda b: (b, 0, 0))
    return pl.pallas_call(
        _attn_kernel,
        grid=(B,),
        in_specs=[seq, seq, seq,
                  pl.BlockSpec((HEAD_GROUPS, GROUP_HEADS * Q_TILE, WINDOW), lambda b: (0, 0, 0)),
                  pl.BlockSpec((1, ATT_WIDTH), lambda b: (0, 0))],
        out_specs=seq,
        out_shape=jax.ShapeDtypeStruct((B, S, ATT_WIDTH), BF16),
        scratch_shapes=[pltpu.VMEM((PAD_ROWS + S, ATT_WIDTH), BF16),
                        pltpu.VMEM((PAD_ROWS + S, ATT_WIDTH), BF16)],
        compiler_params=_params(("arbitrary",)),
        name="attn",
    )(q3, k3, v3, bias_tiles, g_out)


def _outproj_kernel(ma_ref, mb_ref, w_ref, x_ref, g2_ref, wrh_ref, wrl_ref, br_ref,
                    h_ref, xn_ref, route_ref):
    mix = jnp.concatenate([ma_ref[...], mb_ref[...]], axis=1)
    h = x_ref[...] + jnp.dot(mix, w_ref[...], preferred_element_type=F32)
    h_ref[...] = h
    xn = _rms(h, g2_ref[...])
    xn_ref[:, :D_MODEL] = xn

    x_hi = xn.astype(BF16)
    x_lo = (xn - x_hi.astype(F32)).astype(BF16)
    w_hi = wrh_ref[...]
    logits = (jnp.dot(x_hi, w_hi, preferred_element_type=F32)
              + jnp.dot(x_lo, w_hi, preferred_element_type=F32)
              + jnp.dot(x_hi, wrl_ref[...], preferred_element_type=F32)
              + br_ref[...])
    lt = logits.T
    row = lambda j: lt[j:j + 1, :]

    gl = [row(j) for j in range(N_GROUPS)]
    gmax = jnp.maximum(jnp.maximum(gl[0], gl[1]), jnp.maximum(gl[2], gl[3]))
    grp = jnp.where(gl[0] >= gmax, 0, jnp.where(gl[1] >= gmax, 1, jnp.where(gl[2] >= gmax, 2, 3)))
    denom = (jnp.exp(gl[0] - gmax) + jnp.exp(gl[1] - gmax)
             + jnp.exp(gl[2] - gmax) + jnp.exp(gl[3] - gmax))
    p_g = 1.0 / denom

    sel = []
    for j in range(EXPERTS_PER_GROUP):
        cand = [row(N_GROUPS + g * EXPERTS_PER_GROUP + j) for g in range(N_GROUPS)]
        sel.append(jnp.where(grp == 0, cand[0],
                             jnp.where(grp == 1, cand[1], jnp.where(grp == 2, cand[2], cand[3]))))

    def first_argmax(vals):
        top = jnp.maximum(jnp.maximum(vals[0], vals[1]), jnp.maximum(vals[2], vals[3]))
        idx = jnp.where(vals[0] >= top, 0,
                        jnp.where(vals[1] >= top, 1, jnp.where(vals[2] >= top, 2, 3)))
        return top, idx

    v1, i1 = first_argmax(sel)
    rest = [jnp.where(i1 == j, -jnp.inf, sel[j]) for j in range(EXPERTS_PER_GROUP)]
    v2, i2 = first_argmax(rest)
    t = jnp.exp(v2 - v1)
    w1 = p_g * (1.0 / (1.0 + t))
    w2 = p_g * (t / (1.0 + t))
    n_tok = route_ref.shape[1]
    sub = lax.broadcasted_iota(jnp.int32, (ROUTE_ROWS, n_tok), 0)
    table = jnp.zeros((ROUTE_ROWS, n_tok), F32)
    for j in range(EXPERTS_PER_GROUP):
        w_j = jnp.where(i1 == j, w1, 0.0) + jnp.where(i2 == j, w2, 0.0)
        table = jnp.where(sub == j, jnp.broadcast_to(w_j, (ROUTE_ROWS, n_tok)), table)
    route_ref[...] = jnp.where(sub == EXPERTS_PER_GROUP,
                               jnp.broadcast_to(grp.astype(F32), (ROUTE_ROWS, n_tok)), table)
    padded = jnp.concatenate([table, jnp.zeros((ROUTE_LANES - ROUTE_ROWS, n_tok), F32)], axis=0)
    xn_ref[:, D_MODEL:] = padded.T


def _outproj(mix_a, mix_b, w_bf, x2, g2, wr_hi, wr_lo, br):
    T = x2.shape[0]
    row = lambda i: (i, 0)
    const = lambda i: (0, 0)
    full = pl.BlockSpec((ROW_TILE, D_MODEL), row)
    half = pl.BlockSpec((ROW_TILE, REC_WIDTH), row)
    return pl.pallas_call(
        _outproj_kernel,
        grid=(T // ROW_TILE,),
        in_specs=[half, half,
                  pl.BlockSpec((D_MODEL, D_MODEL), const),
                  full,
                  pl.BlockSpec((1, D_MODEL), const),
                  pl.BlockSpec((D_MODEL, ROUTE_LANES), const),
                  pl.BlockSpec((D_MODEL, ROUTE_LANES), const),
                  pl.BlockSpec((1, ROUTE_LANES), const)],
        out_specs=[full, pl.BlockSpec((ROW_TILE, TOKEN_ROW), row),
                   pl.BlockSpec((ROUTE_ROWS, ROW_TILE), lambda i: (0, i))],
        out_shape=[jax.ShapeDtypeStruct((T, D_MODEL), F32),
                   jax.ShapeDtypeStruct((T, TOKEN_ROW), F32),
                   jax.ShapeDtypeStruct((ROUTE_ROWS, T), F32)],
        compiler_params=_params(("arbitrary",)),
        name="outproj",
    )(mix_a, mix_b, w_bf, x2, g2, wr_hi, wr_lo, br)


def _experts_kernel(src_ref, dst_ref, bgroup_ref, x_hbm, wg_ref, wu_ref, wd_ref, out_hbm,
                    xbuf_ref, ybuf_ref, xbf_ref, ws_ref, gsem, ssem):
    del bgroup_ref
    n = EXPERT_BLOCK
    i = pl.program_id(0)
    last = pl.num_programs(0) - 1
    slot = i % 2
    other = 1 - slot

    def gather(tok, r, s):
        return pltpu.make_async_copy(x_hbm.at[pl.ds(tok, 1)], xbuf_ref.at[s, pl.ds(r, 1)], gsem.at[s])

    def scatter(tok, r, s):
        return pltpu.make_async_copy(ybuf_ref.at[s, pl.ds(r, 1)], out_hbm.at[pl.ds(tok, 1)], ssem.at[s])

    def for_rows(fn):
        def body(r, carry):
            fn(r)
            return carry
        lax.fori_loop(0, n, body, 0, unroll=8)

    @pl.when(i == 0)
    def _():
        ybuf_ref[...] = jnp.zeros(ybuf_ref.shape, F32)
        for_rows(lambda r: gather(src_ref[r], r, 0).start())

    for_rows(lambda r: gather(0, r, slot).wait())

    @pl.when(i >= 1)
    def _():
        for_rows(lambda r: scatter(0, r, slot).wait())

    nxt = jnp.minimum(i + 1, last) * n
    prv = i * n
    n_batches = 2 * EXPERTS_PER_GROUP

    def start_copies(k):
        for r in range(k * n // n_batches, (k + 1) * n // n_batches):
            gather(src_ref[nxt + r], r, other).start()
            scatter(dst_ref[prv + r], r, other).start()

    rows = xbuf_ref[slot]
    xbf_ref[...] = rows[:, :D_MODEL].astype(BF16)
    ws_ref[...] = rows[:, D_MODEL:]
    ws = ws_ref[...]

    def anchor():
        t = xbuf_ref[slot, 0:2 * SUBLANES, 0:LANES]
        zero = jnp.minimum(jnp.abs(t), 0.0)
        tile = xbf_ref[0:2 * SUBLANES, 0:LANES].astype(F32)
        xbf_ref[0:2 * SUBLANES, 0:LANES] = (tile + zero).astype(BF16)

    hidden = []
    for j in range(EXPERTS_PER_GROUP):
        gate = jnp.dot(xbf_ref[...], wg_ref[j], preferred_element_type=F32)
        start_copies(2 * j)
        anchor()
        up = jnp.dot(xbf_ref[...], wu_ref[j], preferred_element_type=F32)
        start_copies(2 * j + 1)
        if j + 1 < EXPERTS_PER_GROUP:
            anchor()
        hid = (jax.nn.silu(gate) * up) * ws[:, j:j + 1]
        hidden.append(hid.astype(BF16))
    hcat = jnp.concatenate(hidden, axis=1)
    wd = wd_ref[...].reshape(EXPERTS_PER_GROUP * D_FF_EXPERT, D_MODEL)
    ybuf_ref[slot] = jnp.dot(hcat, wd, preferred_element_type=F32)

    @pl.when(i == last)
    def _():
        for_rows(lambda r: scatter(0, r, other).wait())
        for_rows(lambda r: scatter(dst_ref[prv + n + r], r, slot).start())
        for_rows(lambda r: scatter(0, r, slot).wait())
        for_rows(lambda r: gather(0, r, other).wait())


def _experts(src, dst, bgroup, rows, wg_bf, wu_bf, wd_bf, n_out):
    n_blocks = src.shape[0] // EXPERT_BLOCK
    grp = lambda i, src, dst, bg: (bg[i], 0, 0)
    grid_spec = pltpu.PrefetchScalarGridSpec(
        num_scalar_prefetch=3,
        grid=(n_blocks,),
        in_specs=[pl.BlockSpec(memory_space=pl.ANY),
                  pl.BlockSpec((EXPERTS_PER_GROUP, D_MODEL, D_FF_EXPERT), grp),
                  pl.BlockSpec((EXPERTS_PER_GROUP, D_MODEL, D_FF_EXPERT), grp),
                  pl.BlockSpec((EXPERTS_PER_GROUP, D_FF_EXPERT, D_MODEL), grp)],
        out_specs=pl.BlockSpec(memory_space=pl.ANY),
        scratch_shapes=[pltpu.VMEM((2, EXPERT_BLOCK, TOKEN_ROW), F32),
                        pltpu.VMEM((2, EXPERT_BLOCK, D_MODEL), F32),
                        pltpu.VMEM((EXPERT_BLOCK, D_MODEL), BF16),
                        pltpu.VMEM((EXPERT_BLOCK, ROUTE_LANES), F32),
                        pltpu.SemaphoreType.DMA((2,)),
                        pltpu.SemaphoreType.DMA((2,))],
    )
    return pl.pallas_call(
        _experts_kernel,
        grid_spec=grid_spec,
        out_shape=jax.ShapeDtypeStruct((n_out, D_MODEL), F32),
        compiler_params=_params(("arbitrary",)),
        name="experts",
    )(src, dst, bgroup, rows, wg_bf, wu_bf, wd_bf)


def _final_kernel(h_ref, y_ref, g_ref, o_ref):
    o_ref[...] = _rms(h_ref[...] + y_ref[...], g_ref[...])


def _final(h1, moe, g):
    T = h1.shape[0]
    row = pl.BlockSpec((ROW_TILE, D_MODEL), lambda i: (i, 0))
    return pl.pallas_call(
        _final_kernel,
        grid=(T // ROW_TILE,),
        in_specs=[row, row, pl.BlockSpec((1, D_MODEL), lambda i: (0, 0))],
        out_specs=row,
        out_shape=jax.ShapeDtypeStruct((T, D_MODEL), F32),
        compiler_params=_params(("arbitrary",)),
        name="final",
    )(h1, moe, g)


def _block_diag(w):
    h, d, _ = w.shape
    eye = jnp.eye(h, dtype=w.dtype)
    return (eye[:, None, :, None] * w[:, :, None, :]).reshape(h * d, h * d)


def _bias_table(rel_bias):
    rb = rel_bias.astype(F32)
    n_far = PAD_ROWS - MAX_REL + CHUNK
    far = jnp.broadcast_to(rb[:, 2 * MAX_REL:], (ATT_HEADS, n_far))
    near = rb[:, MAX_REL - CHUNK + 1:2 * MAX_REL][:, ::-1]
    diag = jnp.concatenate([far, near], axis=1)
    bias = jnp.stack([diag[:, CHUNK - 1 - q:CHUNK - 1 - q + BAND] for q in range(CHUNK)], axis=1)
    off = jnp.full((ATT_HEADS, CHUNK, CHUNK), -1e30, F32)
    first = jnp.concatenate([bias, off], axis=2)
    second = jnp.concatenate([off, bias], axis=2)
    tiles = jnp.concatenate([first, second], axis=1)
    return tiles.reshape(HEAD_GROUPS, GROUP_HEADS * Q_TILE, WINDOW)


def _dispatch_plan(grp, T):
    onehot = (grp[:, None] == jnp.arange(N_GROUPS)[None, :]).astype(jnp.int32)
    incl = jnp.cumsum(onehot, axis=0)
    counts = incl[-1]
    padded = (counts + EXPERT_BLOCK - 1) // EXPERT_BLOCK * EXPERT_BLOCK
    pend = jnp.cumsum(padded)
    pstart = pend - padded
    dest = jnp.sum((incl - onehot + pstart[None, :]) * onehot, axis=1).astype(jnp.int32)
    n_pad = T + N_GROUPS * EXPERT_BLOCK
    n_blocks = n_pad // EXPERT_BLOCK
    lane = jnp.arange(n_pad, dtype=jnp.int32) % EXPERT_BLOCK
    slot_row = (T + lane).at[dest].set(jnp.arange(T, dtype=jnp.int32))
    src = jnp.where(slot_row < T, slot_row, 0)
    dummy_block = T + EXPERT_BLOCK + jnp.arange(EXPERT_BLOCK, dtype=jnp.int32)
    dst = jnp.concatenate([dummy_block, slot_row])
    bstart = jnp.arange(n_blocks, dtype=jnp.int32) * EXPERT_BLOCK
    bgroup = jnp.sum((pend[None, :] <= bstart[:, None]).astype(jnp.int32), axis=1)
    bgroup = jnp.minimum(bgroup, N_GROUPS - 1)
    return src, dst, bgroup, T + 2 * EXPERT_BLOCK


def kernel(x, norm1_g, w_in, conv_w, conv_b, w_rg_a, b_rg_a, w_rg_x, b_rg_x, lru_lambda,
           rel_bias, g_rec_out, g_att_out, w_out, norm2_g, w_group, b_group, w_router,
           b_router, w_e_gate, w_e_up, w_e_down, final_g):
    B, S, D = x.shape
    T = B * S
    assert w_in.shape[0] == 1, "single-layer block"
    l = 0
    h = x.reshape(T, D)
    u, gate, q, k, v = _inproj(h, norm1_g[l].reshape(1, D), w_in[l].astype(BF16))

    wab = jnp.concatenate([_block_diag(w_rg_a[l]), _block_diag(w_rg_x[l])], axis=1).astype(BF16)
    bab = jnp.concatenate([b_rg_a[l].reshape(1, REC_WIDTH), b_rg_x[l].reshape(1, REC_WIDTH)], axis=1)
    mix_a = _rglru(u.reshape(B, S, REC_WIDTH), gate.reshape(B, S, REC_WIDTH),
                   conv_w[l], conv_b[l].reshape(1, REC_WIDTH), wab, bab,
                   lru_lambda[l].reshape(1, REC_WIDTH), g_rec_out[l].reshape(1, REC_WIDTH))

    mix_b = _attention(q.reshape(B, S, ATT_WIDTH), k.reshape(B, S, ATT_WIDTH),
                       v.reshape(B, S, ATT_WIDTH), _bias_table(rel_bias[l]),
                       g_att_out[l].reshape(1, ATT_WIDTH))

    n_route = N_GROUPS + N_GROUPS * EXPERTS_PER_GROUP
    wr = jnp.concatenate([w_group[l].astype(F32), w_router[l].astype(F32)], axis=1)
    wr = jnp.pad(wr, ((0, 0), (0, ROUTE_LANES - n_route)))
    wr_hi = wr.astype(BF16)
    wr_lo = (wr - wr_hi.astype(F32)).astype(BF16)
    br = jnp.concatenate([b_group[l].astype(F32), b_router[l].astype(F32)])
    br = jnp.pad(br, (0, ROUTE_LANES - n_route)).reshape(1, ROUTE_LANES)
    h1, xn, route = _outproj(mix_a.reshape(T, REC_WIDTH), mix_b.reshape(T, ATT_WIDTH),
                             w_out[l].astype(BF16), h, norm2_g[l].reshape(1, D),
                             wr_hi, wr_lo, br)

    grp = route[EXPERTS_PER_GROUP].astype(jnp.int32)
    src, dst, bgroup, n_out = _dispatch_plan(grp, T)
    moe = _experts(src, dst, bgroup, xn, w_e_gate[l].astype(BF16),
                   w_e_up[l].astype(BF16), w_e_down[l].astype(BF16), n_out)
    out = _final(h1, moe, final_g.reshape(1, D))
    return out.reshape(B, S, D)
```

```python
import functools

import jax
import jax.numpy as jnp
from jax import lax
from jax.experimental import pallas as pl
from jax.experimental.pallas import tpu as pltpu

F32 = jnp.float32
BF16 = jnp.bfloat16

D_MODEL = 1024
CHUNK = 64
EPS = 1e-6
REC_WIDTH = 512
REC_HEADS = 8
REC_HEAD_DIM = 64
CONV_WIDTH = 4
LRU_C = 8.0
ATT_HEAD_DIM = 64
ATT_HEADS = 8
ATT_WIDTH = 512
LEFT_CHUNKS = 8
BAND = (LEFT_CHUNKS + 1) * CHUNK
MAX_REL = 128
IN_WIDTH = 2 * REC_WIDTH + 3 * ATT_WIDTH
N_GROUPS = 4
EXPERTS_PER_GROUP = 4
D_FF_EXPERT = 512

SUBLANES = 8
LANES = 128
VMEM_LIMIT_BYTES = 56 * 1024 * 1024

ROW_TILE = 512
SCAN_TILE = 256
EXPERT_BLOCK = 256
RING = 3
Q_TILE = 2 * CHUNK
WINDOW = BAND + CHUNK
GROUP_HEADS = 4
GROUP_LANES = GROUP_HEADS * ATT_HEAD_DIM
HEAD_GROUPS = ATT_HEADS // GROUP_HEADS
PAD_ROWS = LEFT_CHUNKS * CHUNK
ROUTE_LANES = LANES
ROUTE_ROWS = SUBLANES
TOKEN_ROW = D_MODEL + ROUTE_LANES


def _params(semantics):
    return pltpu.CompilerParams(dimension_semantics=semantics,
                                vmem_limit_bytes=VMEM_LIMIT_BYTES)


def _rms(x, g):
    ms = jnp.mean(x * x, axis=-1, keepdims=True)
    return (x * lax.rsqrt(ms + EPS)) * g


def _inproj_kernel(x_ref, g_ref, w_ref, u_ref, gate_ref, q_ref, k_ref, v_ref):
    xn = _rms(x_ref[...], g_ref[...])
    z = jnp.dot(xn.astype(BF16), w_ref[...], preferred_element_type=F32)
    u_ref[...] = z[:, :REC_WIDTH]
    gate_ref[...] = z[:, REC_WIDTH:2 * REC_WIDTH]
    o = 2 * REC_WIDTH
    q_ref[...] = (z[:, o:o + ATT_WIDTH] * (ATT_HEAD_DIM ** -0.5)).astype(BF16)
    k_ref[...] = z[:, o + ATT_WIDTH:o + 2 * ATT_WIDTH].astype(BF16)
    v_ref[...] = z[:, o + 2 * ATT_WIDTH:o + 3 * ATT_WIDTH].astype(BF16)


def _inproj(x2, g, w_bf):
    T = x2.shape[0]
    row = lambda i: (i, 0)
    const = lambda i: (0, 0)
    half = pl.BlockSpec((ROW_TILE, REC_WIDTH), row)
    return pl.pallas_call(
        _inproj_kernel,
        grid=(T // ROW_TILE,),
        in_specs=[pl.BlockSpec((ROW_TILE, D_MODEL), row),
                  pl.BlockSpec((1, D_MODEL), const),
                  pl.BlockSpec((D_MODEL, IN_WIDTH), const)],
        out_specs=[half, half, half, half, half],
        out_shape=[jax.ShapeDtypeStruct((T, REC_WIDTH), F32),
                   jax.ShapeDtypeStruct((T, REC_WIDTH), F32),
                   jax.ShapeDtypeStruct((T, ATT_WIDTH), BF16),
                   jax.ShapeDtypeStruct((T, ATT_WIDTH), BF16),
                   jax.ShapeDtypeStruct((T, ATT_WIDTH), BF16)],
        compiler_params=_params(("arbitrary",)),
        name="inproj",
    )(x2, g, w_bf)


def _rglru_kernel(u_ref, gate_ref, convw_ref, convb_ref, wab_ref, bab_ref, lam_ref, gout_ref,
                  o_ref, ext_ref, a_ref, b_ref, carry_ref):
    n = SCAN_TILE

    @pl.when(pl.program_id(1) == 0)
    def _():
        ext_ref[0:SUBLANES, :] = jnp.zeros((SUBLANES, REC_WIDTH), F32)
        carry_ref[...] = jnp.zeros((SUBLANES, REC_WIDTH), F32)

    u = u_ref[...]
    ext_ref[SUBLANES:SUBLANES + n, :] = u
    uc = convb_ref[...]
    for j in range(CONV_WIDTH):
        shift = CONV_WIDTH - 1 - j
        uc = uc + ext_ref[pl.ds(SUBLANES - shift, n), :] * convw_ref[j:j + 1, :]
    ext_ref[0:SUBLANES, :] = u[n - SUBLANES:, :]

    ri = jnp.dot(uc.astype(BF16), wab_ref[...], preferred_element_type=F32) + bab_ref[...]
    r = jax.nn.sigmoid(ri[:, :REC_WIDTH])
    ig = jax.nn.sigmoid(ri[:, REC_WIDTH:])
    log_a = (-LRU_C * r) * jax.nn.softplus(-lam_ref[...])
    a = jnp.exp(log_a)
    bx = jnp.sqrt(-jnp.tanh(log_a) * (a * a + 1.0)) * (ig * uc)

    row8 = lax.broadcasted_iota(jnp.int32, (n, REC_WIDTH), 0) & (SUBLANES - 1)
    for s in (1, 2, 4):
        keep = row8 >= s
        a_prev = jnp.where(keep, pltpu.roll(a, s, 0), 1.0)
        b_prev = jnp.where(keep, pltpu.roll(bx, s, 0), 0.0)
        bx = a * b_prev + bx
        a = a * a_prev
    a_ref[...] = a
    b_ref[...] = bx

    def group(gi, carry):
        r0 = pl.multiple_of(gi * SUBLANES, SUBLANES)
        h = b_ref[pl.ds(r0, SUBLANES), :] + a_ref[pl.ds(r0, SUBLANES), :] * carry
        b_ref[pl.ds(r0, SUBLANES), :] = h
        return jnp.broadcast_to(h[SUBLANES - 1:SUBLANES, :], (SUBLANES, REC_WIDTH))

    carry_ref[...] = lax.fori_loop(0, n // SUBLANES, group, carry_ref[...], unroll=4)

    y = b_ref[...] * jax.nn.gelu(gate_ref[...])
    o_ref[...] = _rms(y, gout_ref[...]).astype(BF16)


def _rglru(u3, gate3, conv_w, conv_b, wab_bf, bab, lam, g_out):
    B, S, _ = u3.shape
    tile = pl.BlockSpec((None, SCAN_TILE, REC_WIDTH), lambda b, i: (b, i, 0))
    const = lambda b, i: (0, 0)
    vec = pl.BlockSpec((1, REC_WIDTH), const)
    return pl.pallas_call(
        _rglru_kernel,
        grid=(B, S // SCAN_TILE),
        in_specs=[tile, tile,
                  pl.BlockSpec((CONV_WIDTH, REC_WIDTH), const), vec,
                  pl.BlockSpec((REC_WIDTH, 2 * REC_WIDTH), const),
                  pl.BlockSpec((1, 2 * REC_WIDTH), const), vec, vec],
        out_specs=tile,
        out_shape=jax.ShapeDtypeStruct((B, S, REC_WIDTH), BF16),
        scratch_shapes=[pltpu.VMEM((SUBLANES + SCAN_TILE, REC_WIDTH), F32),
                        pltpu.VMEM((SCAN_TILE, REC_WIDTH), F32),
                        pltpu.VMEM((SCAN_TILE, REC_WIDTH), F32),
                        pltpu.VMEM((SUBLANES, REC_WIDTH), F32)],
        compiler_params=_params(("arbitrary", "arbitrary")),
        name="rglru",
    )(u3, gate3, conv_w, conv_b, wab_bf, bab, lam, g_out)


def _attn_kernel(q_ref, k_ref, v_ref, bias_ref, gout_ref, o_ref, kpad_ref, vpad_ref):
    S = q_ref.shape[0]
    zeros = jnp.zeros((PAD_ROWS, ATT_WIDTH), BF16)
    kpad_ref[0:PAD_ROWS, :] = zeros
    vpad_ref[0:PAD_ROWS, :] = zeros
    kpad_ref[PAD_ROWS:PAD_ROWS + S, :] = k_ref[...]
    vpad_ref[PAD_ROWS:PAD_ROWS + S, :] = v_ref[...]

    lane_head = lax.broadcasted_iota(jnp.int32, (Q_TILE, GROUP_LANES), 1) // ATT_HEAD_DIM
    key_off = lax.broadcasted_iota(jnp.int32, (GROUP_HEADS * Q_TILE, WINDOW), 1)

    def tile(t, masked):
        r0 = pl.multiple_of(t * Q_TILE, Q_TILE)
        q_rows = q_ref[pl.ds(r0, Q_TILE), :]
        k_win = kpad_ref[pl.ds(r0, WINDOW), :]
        v_win = vpad_ref[pl.ds(r0, WINDOW), :]
        if masked:
            mask = jnp.where(key_off >= PAD_ROWS - t * Q_TILE, 0.0, -1e30)
        outs = []
        for hg in range(HEAD_GROUPS):
            sl = slice(hg * GROUP_LANES, (hg + 1) * GROUP_LANES)
            q4 = q_rows[:, sl]
            zero = jnp.zeros_like(q4)
            q_stack = jnp.concatenate(
                [jnp.where(lane_head == j, q4, zero) for j in range(GROUP_HEADS)], axis=0)
            s = lax.dot_general(q_stack, k_win[:, sl], (((1,), (1,)), ((), ())),
                                preferred_element_type=F32)
            s = s + bias_ref[hg]
            if masked:
                s = s + mask
            m = jnp.max(s, axis=-1, keepdims=True)
            e = jnp.exp(s - m)
            denom = jnp.sum(e, axis=-1, keepdims=True)
            o = jnp.dot(e.astype(BF16), v_win[:, sl], preferred_element_type=F32)
            o = o / denom
            out = o[:Q_TILE]
            for j in range(1, GROUP_HEADS):
                out = jnp.where(lane_head == j, o[j * Q_TILE:(j + 1) * Q_TILE], out)
            outs.append(out)
        y = jnp.concatenate(outs, axis=1)
        o_ref[pl.ds(r0, Q_TILE), :] = _rms(y, gout_ref[...]).astype(BF16)

    def masked_body(t, carry):
        tile(t, True)
        return carry

    def plain_body(t, carry):
        tile(t, False)
        return carry

    lax.fori_loop(0, PAD_ROWS // Q_TILE, masked_body, 0)
    lax.fori_loop(PAD_ROWS // Q_TILE, S // Q_TILE, plain_body, 0)


def _attention(q3, k3, v3, bias_tiles, g_out):
    B, S, _ = q3.shape
    seq = pl.BlockSpec((None, S, ATT_WIDTH), lambda b: (b, 0, 0))
    return pl.pallas_call(
        _attn_kernel,
        grid=(B,),
        in_specs=[seq, seq, seq,
                  pl.BlockSpec((HEAD_GROUPS, GROUP_HEADS * Q_TILE, WINDOW), lambda b: (0, 0, 0)),
                  pl.BlockSpec((1, ATT_WIDTH), lambda b: (0, 0))],
        out_specs=seq,
        out_shape=jax.ShapeDtypeStruct((B, S, ATT_WIDTH), BF16),
        scratch_shapes=[pltpu.VMEM((PAD_ROWS + S, ATT_WIDTH), BF16),
                        pltpu.VMEM((PAD_ROWS + S, ATT_WIDTH), BF16)],
        compiler_params=_params(("arbitrary",)),
        name="attn",
    )(q3, k3, v3, bias_tiles, g_out)


def _outproj_kernel(ma_ref, mb_ref, w_ref, x_ref, g2_ref, wrh_ref, wrl_ref, br_ref,
                    h_ref, xn_ref, route_ref):
    mix = jnp.concatenate([ma_ref[...], mb_ref[...]], axis=1)
    h = x_ref[...] + jnp.dot(mix, w_ref[...], preferred_element_type=F32)
    h_ref[...] = h
    xn = _rms(h, g2_ref[...])
    xn_ref[:, :D_MODEL] = xn

    x_hi = xn.astype(BF16)
    x_lo = (xn - x_hi.astype(F32)).astype(BF16)
    w_hi = wrh_ref[...]
    logits = (jnp.dot(x_hi, w_hi, preferred_element_type=F32)
              + jnp.dot(x_lo, w_hi, preferred_element_type=F32)
              + jnp.dot(x_hi, wrl_ref[...], preferred_element_type=F32)
              + br_ref[...])
    lt = logits.T
    row = lambda j: lt[j:j + 1, :]

    gl = [row(j) for j in range(N_GROUPS)]
    gmax = jnp.maximum(jnp.maximum(gl[0], gl[1]), jnp.maximum(gl[2], gl[3]))
    grp = jnp.where(gl[0] >= gmax, 0, jnp.where(gl[1] >= gmax, 1, jnp.where(gl[2] >= gmax, 2, 3)))
    denom = (jnp.exp(gl[0] - gmax) + jnp.exp(gl[1] - gmax)
             + jnp.exp(gl[2] - gmax) + jnp.exp(gl[3] - gmax))
    p_g = 1.0 / denom

    sel = []
    for j in range(EXPERTS_PER_GROUP):
        cand = [row(N_GROUPS + g * EXPERTS_PER_GROUP + j) for g in range(N_GROUPS)]
        sel.append(jnp.where(grp == 0, cand[0],
                             jnp.where(grp == 1, cand[1], jnp.where(grp == 2, cand[2], cand[3]))))

    def first_argmax(vals):
        top = jnp.maximum(jnp.maximum(vals[0], vals[1]), jnp.maximum(vals[2], vals[3]))
        idx = jnp.where(vals[0] >= top, 0,
                        jnp.where(vals[1] >= top, 1, jnp.where(vals[2] >= top, 2, 3)))
        return top, idx

    v1, i1 = first_argmax(sel)
    rest = [jnp.where(i1 == j, -jnp.inf, sel[j]) for j in range(EXPERTS_PER_GROUP)]
    v2, i2 = first_argmax(rest)
    t = jnp.exp(v2 - v1)
    w1 = p_g * (1.0 / (1.0 + t))
    w2 = p_g * (t / (1.0 + t))
    n_tok = route_ref.shape[1]
    sub = lax.broadcasted_iota(jnp.int32, (ROUTE_ROWS, n_tok), 0)
    table = jnp.zeros((ROUTE_ROWS, n_tok), F32)
    for j in range(EXPERTS_PER_GROUP):
        w_j = jnp.where(i1 == j, w1, 0.0) + jnp.where(i2 == j, w2, 0.0)
        table = jnp.where(sub == j, jnp.broadcast_to(w_j, (ROUTE_ROWS, n_tok)), table)
    route_ref[...] = jnp.where(sub == EXPERTS_PER_GROUP,
                               jnp.broadcast_to(grp.astype(F32), (ROUTE_ROWS, n_tok)), table)
    padded = jnp.concatenate([table, jnp.zeros((ROUTE_LANES - ROUTE_ROWS, n_tok), F32)], axis=0)
    xn_ref[:, D_MODEL:] = padded.T


def _outproj(mix_a, mix_b, w_bf, x2, g2, wr_hi, wr_lo, br):
    T = x2.shape[0]
    row = lambda i: (i, 0)
    const = lambda i: (0, 0)
    full = pl.BlockSpec((ROW_TILE, D_MODEL), row)
    half = pl.BlockSpec((ROW_TILE, REC_WIDTH), row)
    return pl.pallas_call(
        _outproj_kernel,
        grid=(T // ROW_TILE,),
        in_specs=[half, half,
                  pl.BlockSpec((D_MODEL, D_MODEL), const),
                  full,
                  pl.BlockSpec((1, D_MODEL), const),
                  pl.BlockSpec((D_MODEL, ROUTE_LANES), const),
                  pl.BlockSpec((D_MODEL, ROUTE_LANES), const),
                  pl.BlockSpec((1, ROUTE_LANES), const)],
        out_specs=[full, pl.BlockSpec((ROW_TILE, TOKEN_ROW), row),
                   pl.BlockSpec((ROUTE_ROWS, ROW_TILE), lambda i: (0, i))],
        out_shape=[jax.ShapeDtypeStruct((T, D_MODEL), F32),
                   jax.ShapeDtypeStruct((T, TOKEN_ROW), F32),
                   jax.ShapeDtypeStruct((ROUTE_ROWS, T), F32)],
        compiler_params=_params(("arbitrary",)),
        name="outproj",
    )(mix_a, mix_b, w_bf, x2, g2, wr_hi, wr_lo, br)


def _experts_kernel(src_ref, dst_ref, bgroup_ref, x_hbm, wg_ref, wu_ref, wd_ref, out_hbm,
                    xbuf_ref, ybuf_ref, xbf_ref, ws_ref, gsem, ssem):
    del bgroup_ref
    n = EXPERT_BLOCK
    i = pl.program_id(0)
    last = pl.num_programs(0) - 1
    slot = i % RING
    ahead = (i + 2) % RING
    before = (i + 1) % RING

    def gather(tok, r, s):
        return pltpu.make_async_copy(x_hbm.at[pl.ds(tok, 1)], xbuf_ref.at[s, pl.ds(r, 1)], gsem.at[s])

    def scatter(tok, r, s):
        return pltpu.make_async_copy(ybuf_ref.at[s, pl.ds(r, 1)], out_hbm.at[pl.ds(tok, 1)], ssem.at[s])

    def for_rows(fn):
        def body(r, carry):
            fn(r)
            return carry
        lax.fori_loop(0, n, body, 0, unroll=8)

    @pl.when(i == 0)
    def _():
        ybuf_ref[...] = jnp.zeros(ybuf_ref.shape, F32)
        for_rows(lambda r: gather(src_ref[r], r, 0).start())
        for_rows(lambda r: gather(src_ref[n + r], r, 1).start())

    for_rows(lambda r: gather(0, r, slot).wait())

    @pl.when(i >= 2)
    def _():
        for_rows(lambda r: scatter(0, r, slot).wait())

    nxt = jnp.minimum(i + 2, last) * n
    prv = i * n
    n_batches = 2 * EXPERTS_PER_GROUP

    def start_copies(k):
        for r in range(k * n // n_batches, (k + 1) * n // n_batches):
            gather(src_ref[nxt + r], r, ahead).start()
            scatter(dst_ref[prv + r], r, ahead).start()

    rows = xbuf_ref[slot]
    xbf_ref[...] = rows[:, :D_MODEL].astype(BF16)
    ws_ref[...] = rows[:, D_MODEL:]
    ws = ws_ref[...]

    def anchor():
        t = xbuf_ref[slot, 0:2 * SUBLANES, 0:LANES]
        zero = jnp.minimum(jnp.abs(t), 0.0)
        tile = xbf_ref[0:2 * SUBLANES, 0:LANES].astype(F32)
        xbf_ref[0:2 * SUBLANES, 0:LANES] = (tile + zero).astype(BF16)

    hidden = []
    for j in range(EXPERTS_PER_GROUP):
        gate = jnp.dot(xbf_ref[...], wg_ref[j], preferred_element_type=F32)
        start_copies(2 * j)
        anchor()
        up = jnp.dot(xbf_ref[...], wu_ref[j], preferred_element_type=F32)
        start_copies(2 * j + 1)
        if j + 1 < EXPERTS_PER_GROUP:
            anchor()
        hid = (jax.nn.silu(gate) * up) * ws[:, j:j + 1]
        hidden.append(hid.astype(BF16))
    hcat = jnp.concatenate(hidden, axis=1)
    wd = wd_ref[...].reshape(EXPERTS_PER_GROUP * D_FF_EXPERT, D_MODEL)
    ybuf_ref[slot] = jnp.dot(hcat, wd, preferred_element_type=F32)

    @pl.when(i == last)
    def _():
        for_rows(lambda r: scatter(0, r, before).wait())
        for_rows(lambda r: scatter(dst_ref[prv + n + r], r, slot).start())
        for_rows(lambda r: scatter(0, r, ahead).wait())
        for_rows(lambda r: scatter(0, r, slot).wait())
        for_rows(lambda r: gather(0, r, ahead).wait())
        for_rows(lambda r: gather(0, r, before).wait())


def _experts(src, dst, bgroup, rows, wg_bf, wu_bf, wd_bf, n_out):
    n_blocks = src.shape[0] // EXPERT_BLOCK
    grp = lambda i, src, dst, bg: (bg[i], 0, 0)
    grid_spec = pltpu.PrefetchScalarGridSpec(
        num_scalar_prefetch=3,
        grid=(n_blocks,),
        in_specs=[pl.BlockSpec(memory_space=pl.ANY),
                  pl.BlockSpec((EXPERTS_PER_GROUP, D_MODEL, D_FF_EXPERT), grp),
                  pl.BlockSpec((EXPERTS_PER_GROUP, D_MODEL, D_FF_EXPERT), grp),
                  pl.BlockSpec((EXPERTS_PER_GROUP, D_FF_EXPERT, D_MODEL), grp)],
        out_specs=pl.BlockSpec(memory_space=pl.ANY),
        scratch_shapes=[pltpu.VMEM((RING, EXPERT_BLOCK, TOKEN_ROW), F32),
                        pltpu.VMEM((RING, EXPERT_BLOCK, D_MODEL), F32),
                        pltpu.VMEM((EXPERT_BLOCK, D_MODEL), BF16),
                        pltpu.VMEM((EXPERT_BLOCK, ROUTE_LANES), F32),
                        pltpu.SemaphoreType.DMA((RING,)),
                        pltpu.SemaphoreType.DMA((RING,))],
    )
    return pl.pallas_call(
        _experts_kernel,
        grid_spec=grid_spec,
        out_shape=jax.ShapeDtypeStruct((n_out, D_MODEL), F32),
        compiler_params=_params(("arbitrary",)),
        name="experts",
    )(src, dst, bgroup, rows, wg_bf, wu_bf, wd_bf)


def _final_kernel(h_ref, y_ref, g_ref, o_ref):
    o_ref[...] = _rms(h_ref[...] + y_ref[...], g_ref[...])


def _final(h1, moe, g):
    T = h1.shape[0]
    row = pl.BlockSpec((ROW_TILE, D_MODEL), lambda i: (i, 0))
    return pl.pallas_call(
        _final_kernel,
        grid=(T // ROW_TILE,),
        in_specs=[row, row, pl.BlockSpec((1, D_MODEL), lambda i: (0, 0))],
        out_specs=row,
        out_shape=jax.ShapeDtypeStruct((T, D_MODEL), F32),
        compiler_params=_params(("arbitrary",)),
        name="final",
    )(h1, moe, g)


def _block_diag(w):
    h, d, _ = w.shape
    eye = jnp.eye(h, dtype=w.dtype)
    return (eye[:, None, :, None] * w[:, :, None, :]).reshape(h * d, h * d)


def _bias_table(rel_bias):
    rb = rel_bias.astype(F32)
    n_far = PAD_ROWS - MAX_REL + CHUNK
    far = jnp.broadcast_to(rb[:, 2 * MAX_REL:], (ATT_HEADS, n_far))
    near = rb[:, MAX_REL - CHUNK + 1:2 * MAX_REL][:, ::-1]
    diag = jnp.concatenate([far, near], axis=1)
    bias = jnp.stack([diag[:, CHUNK - 1 - q:CHUNK - 1 - q + BAND] for q in range(CHUNK)], axis=1)
    off = jnp.full((ATT_HEADS, CHUNK, CHUNK), -1e30, F32)
    first = jnp.concatenate([bias, off], axis=2)
    second = jnp.concatenate([off, bias], axis=2)
    tiles = jnp.concatenate([first, second], axis=1)
    return tiles.reshape(HEAD_GROUPS, GROUP_HEADS * Q_TILE, WINDOW)


def _dispatch_plan(grp, T):
    onehot = (grp[:, None] == jnp.arange(N_GROUPS)[None, :]).astype(jnp.int32)
    incl = jnp.cumsum(onehot, axis=0)
    counts = incl[-1]
    padded = (counts + EXPERT_BLOCK - 1) // EXPERT_BLOCK * EXPERT_BLOCK
    pend = jnp.cumsum(padded)
    pstart = pend - padded
    dest = jnp.sum((incl - onehot + pstart[None, :]) * onehot, axis=1).astype(jnp.int32)
    n_pad = T + N_GROUPS * EXPERT_BLOCK
    n_blocks = n_pad // EXPERT_BLOCK
    assert n_blocks >= RING
    pos = jnp.arange(n_pad, dtype=jnp.int32)
    dummy = T + ((pos // EXPERT_BLOCK) % 2) * EXPERT_BLOCK + pos % EXPERT_BLOCK
    slot_row = dummy.at[dest].set(jnp.arange(T, dtype=jnp.int32))
    src = jnp.where(slot_row < T, slot_row, 0)
    dummy_block = T + 2 * EXPERT_BLOCK + jnp.arange(EXPERT_BLOCK, dtype=jnp.int32)
    dst = jnp.concatenate([dummy_block, slot_row])
    bstart = jnp.arange(n_blocks, dtype=jnp.int32) * EXPERT_BLOCK
    bgroup = jnp.sum((pend[None, :] <= bstart[:, None]).astype(jnp.int32), axis=1)
    bgroup = jnp.minimum(bgroup, N_GROUPS - 1)
    return src, dst, bgroup, T + 3 * EXPERT_BLOCK


def kernel(x, norm1_g, w_in, conv_w, conv_b, w_rg_a, b_rg_a, w_rg_x, b_rg_x, lru_lambda,
           rel_bias, g_rec_out, g_att_out, w_out, norm2_g, w_group, b_group, w_router,
           b_router, w_e_gate, w_e_up, w_e_down, final_g):
    B, S, D = x.shape
    T = B * S
    assert w_in.shape[0] == 1, "single-layer block"
    l = 0
    h = x.reshape(T, D)
    u, gate, q, k, v = _inproj(h, norm1_g[l].reshape(1, D), w_in[l].astype(BF16))

    wab = jnp.concatenate([_block_diag(w_rg_a[l]), _block_diag(w_rg_x[l])], axis=1).astype(BF16)
    bab = jnp.concatenate([b_rg_a[l].reshape(1, REC_WIDTH), b_rg_x[l].reshape(1, REC_WIDTH)], axis=1)
    mix_a = _rglru(u.reshape(B, S, REC_WIDTH), gate.reshape(B, S, REC_WIDTH),
                   conv_w[l], conv_b[l].reshape(1, REC_WIDTH), wab, bab,
                   lru_lambda[l].reshape(1, REC_WIDTH), g_rec_out[l].reshape(1, REC_WIDTH))

    mix_b = _attention(q.reshape(B, S, ATT_WIDTH), k.reshape(B, S, ATT_WIDTH),
                       v.reshape(B, S, ATT_WIDTH), _bias_table(rel_bias[l]),
                       g_att_out[l].reshape(1, ATT_WIDTH))

    n_route = N_GROUPS + N_GROUPS * EXPERTS_PER_GROUP
    wr = jnp.concatenate([w_group[l].astype(F32), w_router[l].astype(F32)], axis=1)
    wr = jnp.pad(wr, ((0, 0), (0, ROUTE_LANES - n_route)))
    wr_hi = wr.astype(BF16)
    wr_lo = (wr - wr_hi.astype(F32)).astype(BF16)
    br = jnp.concatenate([b_group[l].astype(F32), b_router[l].astype(F32)])
    br = jnp.pad(br, (0, ROUTE_LANES - n_route)).reshape(1, ROUTE_LANES)
    h1, xn, route = _outproj(mix_a.reshape(T, REC_WIDTH), mix_b.reshape(T, ATT_WIDTH),
                             w_out[l].astype(BF16), h, norm2_g[l].reshape(1, D),
                             wr_hi, wr_lo, br)

    grp = route[EXPERTS_PER_GROUP].astype(jnp.int32)
    src, dst, bgroup, n_out = _dispatch_plan(grp, T)
    moe = _experts(src, dst, bgroup, xn, w_e_gate[l].astype(BF16),
                   w_e_up[l].astype(BF16), w_e_down[l].astype(BF16), n_out)
    out = _final(h1, moe, final_g.reshape(1, D))
    return out.reshape(B, S, D)
```

```python
import functools

import jax
import jax.numpy as jnp
from jax import lax
from jax.experimental import pallas as pl
from jax.experimental.pallas import tpu as pltpu

F32 = jnp.float32
BF16 = jnp.bfloat16

D_MODEL = 1024
CHUNK = 64
EPS = 1e-6
REC_WIDTH = 512
CONV_WIDTH = 4
LRU_C = 8.0
ATT_HEAD_DIM = 64
ATT_HEADS = 8
ATT_WIDTH = 512
LEFT_CHUNKS = 8
BAND = (LEFT_CHUNKS + 1) * CHUNK
MAX_REL = 128
IN_WIDTH = 2 * REC_WIDTH + 3 * ATT_WIDTH
N_GROUPS = 4
EXPERTS_PER_GROUP = 4
D_FF_EXPERT = 512

SUBLANES = 8
LANES = 128
VMEM_LIMIT_BYTES = 56 * 1024 * 1024

ROW_TILE = 512
SCAN_TILE = 512
EXPERT_BLOCK = 256
MOVE_TILE = 512
RING = 3
MOVE_RING = 4
Q_TILE = 2 * CHUNK
WINDOW = BAND + CHUNK
GROUP_HEADS = 4
GROUP_LANES = GROUP_HEADS * ATT_HEAD_DIM
HEAD_GROUPS = ATT_HEADS // GROUP_HEADS
PAD_ROWS = LEFT_CHUNKS * CHUNK
ROUTE_LANES = LANES
ROUTE_ROWS = SUBLANES
TOKEN_ROW = D_MODEL + ROUTE_LANES


def _params(semantics):
    return pltpu.CompilerParams(dimension_semantics=semantics, vmem_limit_bytes=VMEM_LIMIT_BYTES)


def _rms(x, g):
    ms = jnp.mean(x * x, axis=-1, keepdims=True)
    return (x * lax.rsqrt(ms + EPS)) * g


def _inproj_kernel(x_ref, g_ref, w_ref, *rest):
    n = (len(rest) - 6) // 2
    riders_in, outs, riders_out, wbf_ref = rest[:n], rest[n:n + 5], rest[n + 5:2 * n + 5], rest[-1]
    u_ref, gate_ref, q_ref, k_ref, v_ref = outs

    @pl.when(pl.program_id(0) == 0)
    def _():
        wbf_ref[...] = w_ref[...].astype(BF16)

    xn = _rms(x_ref[...], g_ref[...])
    z = jnp.dot(xn.astype(BF16), wbf_ref[...], preferred_element_type=F32)
    u_ref[...] = z[:, :REC_WIDTH]
    gate_ref[...] = z[:, REC_WIDTH:2 * REC_WIDTH]
    o = 2 * REC_WIDTH
    q_ref[...] = (z[:, o:o + ATT_WIDTH] * (ATT_HEAD_DIM ** -0.5)).astype(BF16)
    k_ref[...] = z[:, o + ATT_WIDTH:o + 2 * ATT_WIDTH].astype(BF16)
    v_ref[...] = z[:, o + 2 * ATT_WIDTH:o + 3 * ATT_WIDTH].astype(BF16)
    for src, dst in zip(riders_in, riders_out):
        dst[...] = src[...].astype(BF16)


def _rider_specs(arrays, n_steps, index_map):
    specs = [pl.BlockSpec((a.shape[0] // n_steps, a.shape[1]), index_map) for a in arrays]
    shapes = [jax.ShapeDtypeStruct(a.shape, BF16) for a in arrays]
    return specs, shapes


def _inproj(x2, g, w, riders):
    T = x2.shape[0]
    n_steps = T // ROW_TILE
    row = lambda i: (i, 0)
    const = lambda i: (0, 0)
    half = pl.BlockSpec((ROW_TILE, REC_WIDTH), row)
    rider_specs, rider_shapes = _rider_specs(riders, n_steps, row)
    return pl.pallas_call(
        _inproj_kernel,
        grid=(n_steps,),
        in_specs=[pl.BlockSpec((ROW_TILE, D_MODEL), row),
                  pl.BlockSpec((1, D_MODEL), const),
                  pl.BlockSpec((D_MODEL, IN_WIDTH), const)] + rider_specs,
        out_specs=[half, half, half, half, half] + rider_specs,
        out_shape=[jax.ShapeDtypeStruct((T, REC_WIDTH), F32),
                   jax.ShapeDtypeStruct((T, REC_WIDTH), F32),
                   jax.ShapeDtypeStruct((T, ATT_WIDTH), BF16),
                   jax.ShapeDtypeStruct((T, ATT_WIDTH), BF16),
                   jax.ShapeDtypeStruct((T, ATT_WIDTH), BF16)] + rider_shapes,
        scratch_shapes=[pltpu.VMEM((D_MODEL, IN_WIDTH), BF16)],
        compiler_params=_params(("arbitrary",)),
        name="inproj",
    )(x2, g, w, *riders)


def _rglru_kernel(u_ref, gate_ref, convw_ref, convb_ref, wab_ref, bab_ref, lam_ref, gout_ref, *rest):
    n_riders = (len(rest) - 5) // 2
    riders_in, o_ref = rest[:n_riders], rest[n_riders]
    riders_out = rest[n_riders + 1:2 * n_riders + 1]
    tail_ref, a_ref, b_ref, carry_ref = rest[-4:]
    for src, dst in zip(riders_in, riders_out):
        dst[...] = src[...].astype(BF16)
    n = SCAN_TILE

    @pl.when(pl.program_id(1) == 0)
    def _():
        tail_ref[...] = jnp.zeros((SUBLANES, REC_WIDTH), F32)
        carry_ref[...] = jnp.zeros((SUBLANES, REC_WIDTH), F32)

    groups = n // SUBLANES
    shape3 = (groups, SUBLANES, REC_WIDTH)
    row8 = lax.broadcasted_iota(jnp.int32, shape3, 1)
    u3 = u_ref[...].reshape(shape3)
    ext3 = jnp.concatenate([tail_ref[...].reshape(1, SUBLANES, REC_WIDTH), u3], axis=0)
    tail_ref[...] = u3[groups - 1]
    uc3 = convb_ref[...].reshape(1, 1, REC_WIDTH)
    for j in range(CONV_WIDTH):
        shift = CONV_WIDTH - 1 - j
        if shift:
            rolled = pltpu.roll(ext3, shift, 1)
            x = jnp.where(row8 >= shift, rolled[1:], rolled[:-1])
        else:
            x = u3
        uc3 = uc3 + x * convw_ref[j:j + 1, :].reshape(1, 1, REC_WIDTH)
    uc = uc3.reshape(n, REC_WIDTH)

    ri = jnp.dot(uc.astype(BF16), wab_ref[...], preferred_element_type=F32) + bab_ref[...]
    r = jax.nn.sigmoid(ri[:, :REC_WIDTH])
    ig = jax.nn.sigmoid(ri[:, REC_WIDTH:])
    log_a = (-LRU_C * r) * jax.nn.softplus(-lam_ref[...])
    a = jnp.exp(log_a)
    bx = jnp.sqrt(-jnp.tanh(log_a) * (a * a + 1.0)) * (ig * uc)

    a = a.reshape(shape3)
    bx = bx.reshape(shape3)
    for s in (1, 2, 4):
        keep = row8 >= s
        a_prev = jnp.where(keep, pltpu.roll(a, s, 1), 1.0)
        b_prev = jnp.where(keep, pltpu.roll(bx, s, 1), 0.0)
        bx = a * b_prev + bx
        a = a * a_prev
    a_ref[...] = a.reshape(n, REC_WIDTH)
    b_ref[...] = bx.reshape(n, REC_WIDTH)

    def group(gi, carry):
        r0 = pl.multiple_of(gi * SUBLANES, SUBLANES)
        h = b_ref[pl.ds(r0, SUBLANES), :] + a_ref[pl.ds(r0, SUBLANES), :] * carry
        b_ref[pl.ds(r0, SUBLANES), :] = h
        return jnp.broadcast_to(h[SUBLANES - 1:SUBLANES, :], (SUBLANES, REC_WIDTH))

    carry_ref[...] = lax.fori_loop(0, n // SUBLANES, group, carry_ref[...], unroll=4)

    y = b_ref[...] * jax.nn.gelu(gate_ref[...])
    o_ref[...] = _rms(y, gout_ref[...]).astype(BF16)


def _rglru(u3, gate3, conv_w, conv_b, wab_bf, bab, lam, g_out, riders):
    B, S, _ = u3.shape
    n_tiles = S // SCAN_TILE
    tile = pl.BlockSpec((None, SCAN_TILE, REC_WIDTH), lambda b, i: (b, i, 0))
    const = lambda b, i: (0, 0)
    vec = pl.BlockSpec((1, REC_WIDTH), const)
    rider_specs, rider_shapes = _rider_specs(riders, B * n_tiles, lambda b, i: (b * n_tiles + i, 0))
    return pl.pallas_call(
        _rglru_kernel,
        grid=(B, n_tiles),
        in_specs=[tile, tile,
                  pl.BlockSpec((CONV_WIDTH, REC_WIDTH), const), vec,
                  pl.BlockSpec((REC_WIDTH, 2 * REC_WIDTH), const),
                  pl.BlockSpec((1, 2 * REC_WIDTH), const), vec, vec] + rider_specs,
        out_specs=[tile] + rider_specs,
        out_shape=[jax.ShapeDtypeStruct((B, S, REC_WIDTH), BF16)] + rider_shapes,
        scratch_shapes=[pltpu.VMEM((SUBLANES, REC_WIDTH), F32),
                        pltpu.VMEM((SCAN_TILE, REC_WIDTH), F32),
                        pltpu.VMEM((SCAN_TILE, REC_WIDTH), F32),
                        pltpu.VMEM((SUBLANES, REC_WIDTH), F32)],
        compiler_params=_params(("arbitrary", "arbitrary")),
        name="rglru",
    )(u3, gate3, conv_w, conv_b, wab_bf, bab, lam, g_out, *riders)


def _attn_kernel(q_ref, k_ref, v_ref, bias_ref, gout_ref, o_ref):
    S = q_ref.shape[0]
    lane_head = lax.broadcasted_iota(jnp.int32, (Q_TILE, GROUP_LANES), 1) // ATT_HEAD_DIM

    def tile(t, skip):
        r0 = t * Q_TILE
        k0 = r0 - PAD_ROWS + skip
        if not isinstance(t, int):
            r0 = pl.multiple_of(r0, Q_TILE)
            k0 = pl.multiple_of(k0, Q_TILE)
        q_rows = q_ref[pl.ds(r0, Q_TILE), :]
        k_win = k_ref[pl.ds(k0, WINDOW - skip), :]
        v_win = v_ref[pl.ds(k0, WINDOW - skip), :]
        outs = []
        for hg in range(HEAD_GROUPS):
            sl = slice(hg * GROUP_LANES, (hg + 1) * GROUP_LANES)
            q4 = q_rows[:, sl]
            zero = jnp.zeros_like(q4)
            q_stack = jnp.concatenate(
                [jnp.where(lane_head == j, q4, zero) for j in range(GROUP_HEADS)], axis=0)
            s = lax.dot_general(q_stack, k_win[:, sl], (((1,), (1,)), ((), ())),
                                preferred_element_type=F32)
            s = s + bias_ref[hg][:, skip:]
            m = jnp.max(s, axis=-1, keepdims=True)
            e = jnp.exp(s - m)
            denom = jnp.sum(e, axis=-1, keepdims=True)
            o = jnp.dot(e.astype(BF16), v_win[:, sl], preferred_element_type=F32)
            o = o / denom
            out = o[:Q_TILE]
            for j in range(1, GROUP_HEADS):
                out = jnp.where(lane_head == j, o[j * Q_TILE:(j + 1) * Q_TILE], out)
            outs.append(out)
        y = jnp.concatenate(outs, axis=1)
        o_ref[pl.ds(r0, Q_TILE), :] = _rms(y, gout_ref[...]).astype(BF16)

    first_full = PAD_ROWS // Q_TILE
    for t in range(first_full):
        tile(t, PAD_ROWS - t * Q_TILE)

    def full_body(t, carry):
        tile(t, 0)
        return carry

    lax.fori_loop(first_full, S // Q_TILE, full_body, 0, unroll=6)


def _attention(q3, k3, v3, bias_tiles, g_out):
    B, S, _ = q3.shape
    seq = pl.BlockSpec((None, S, ATT_WIDTH), lambda b: (b, 0, 0))
    return pl.pallas_call(
        _attn_kernel,
        grid=(B,),
        in_specs=[seq, seq, seq,
                  pl.BlockSpec((HEAD_GROUPS, GROUP_HEADS * Q_TILE, WINDOW), lambda b: (0, 0, 0)),
                  pl.BlockSpec((1, ATT_WIDTH), lambda b: (0, 0))],
        out_specs=seq,
        out_shape=jax.ShapeDtypeStruct((B, S, ATT_WIDTH), BF16),
        compiler_params=_params(("arbitrary",)),
        name="attn",
    )(q3, k3, v3, bias_tiles, g_out)


def _route(lt):
    n_tok = lt.shape[1]
    row = lambda j: lt[j:j + 1, :]

    gl = [row(j) for j in range(N_GROUPS)]
    gmax = jnp.maximum(jnp.maximum(gl[0], gl[1]), jnp.maximum(gl[2], gl[3]))
    grp = jnp.where(gl[0] >= gmax, 0, jnp.where(gl[1] >= gmax, 1, jnp.where(gl[2] >= gmax, 2, 3)))
    denom = (jnp.exp(gl[0] - gmax) + jnp.exp(gl[1] - gmax)
             + jnp.exp(gl[2] - gmax) + jnp.exp(gl[3] - gmax))
    p_g = 1.0 / denom

    sel = []
    for j in range(EXPERTS_PER_GROUP):
        cand = [row(N_GROUPS + g * EXPERTS_PER_GROUP + j) for g in range(N_GROUPS)]
        sel.append(jnp.where(grp == 0, cand[0],
                             jnp.where(grp == 1, cand[1], jnp.where(grp == 2, cand[2], cand[3]))))

    def first_argmax(vals):
        top = jnp.maximum(jnp.maximum(vals[0], vals[1]), jnp.maximum(vals[2], vals[3]))
        idx = jnp.where(vals[0] >= top, 0,
                        jnp.where(vals[1] >= top, 1, jnp.where(vals[2] >= top, 2, 3)))
        return top, idx

    v1, i1 = first_argmax(sel)
    rest = [jnp.where(i1 == j, -jnp.inf, sel[j]) for j in range(EXPERTS_PER_GROUP)]
    v2, i2 = first_argmax(rest)
    t = jnp.exp(v2 - v1)
    w1 = p_g * (1.0 / (1.0 + t))
    w2 = p_g * (t / (1.0 + t))
    sub = lax.broadcasted_iota(jnp.int32, (ROUTE_ROWS, n_tok), 0)
    table = jnp.zeros((ROUTE_ROWS, n_tok), F32)
    for j in range(EXPERTS_PER_GROUP):
        w_j = jnp.where(i1 == j, w1, 0.0) + jnp.where(i2 == j, w2, 0.0)
        table = jnp.where(sub == j, jnp.broadcast_to(w_j, (ROUTE_ROWS, n_tok)), table)
    return table, grp


def _outproj_kernel(ma_ref, mb_ref, w_ref, x_ref, g2_ref, wr_ref, br_ref,
                    h_ref, xn_ref, route_ref):
    mix = jnp.concatenate([ma_ref[...], mb_ref[...]], axis=1)
    h = x_ref[...] + jnp.dot(mix, w_ref[...], preferred_element_type=F32)
    h_ref[...] = h
    xn = _rms(h, g2_ref[...])
    xn_ref[:, :D_MODEL] = xn

    x_hi = xn.astype(BF16)
    x_lo = (xn - x_hi.astype(F32)).astype(BF16)
    wr = wr_ref[...]
    both = jnp.dot(x_hi, wr, preferred_element_type=F32)
    logits = (both[:, :ROUTE_LANES] + both[:, ROUTE_LANES:]
              + jnp.dot(x_lo, wr[:, :ROUTE_LANES], preferred_element_type=F32)
              + br_ref[...])
    table, grp = _route(logits.T)
    n_tok = route_ref.shape[1]
    sub = lax.broadcasted_iota(jnp.int32, (ROUTE_ROWS, n_tok), 0)
    route_ref[...] = jnp.where(sub == EXPERTS_PER_GROUP,
                               jnp.broadcast_to(grp.astype(F32), (ROUTE_ROWS, n_tok)), table)
    padded = jnp.concatenate([table, jnp.zeros((ROUTE_LANES - ROUTE_ROWS, n_tok), F32)], axis=0)
    xn_ref[:, D_MODEL:] = padded.T


def _outproj(mix_a, mix_b, w_bf, x2, g2, wr_split, br):
    T = x2.shape[0]
    row = lambda i: (i, 0)
    const = lambda i: (0, 0)
    full = pl.BlockSpec((ROW_TILE, D_MODEL), row)
    half = pl.BlockSpec((ROW_TILE, REC_WIDTH), row)
    return pl.pallas_call(
        _outproj_kernel,
        grid=(T // ROW_TILE,),
        in_specs=[half, half,
                  pl.BlockSpec((D_MODEL, D_MODEL), const),
                  full,
                  pl.BlockSpec((1, D_MODEL), const),
                  pl.BlockSpec((D_MODEL, 2 * ROUTE_LANES), const),
                  pl.BlockSpec((1, ROUTE_LANES), const)],
        out_specs=[full, pl.BlockSpec((ROW_TILE, TOKEN_ROW), row),
                   pl.BlockSpec((ROUTE_ROWS, ROW_TILE), lambda i: (0, i))],
        out_shape=[jax.ShapeDtypeStruct((T, D_MODEL), F32),
                   jax.ShapeDtypeStruct((T, TOKEN_ROW), F32),
                   jax.ShapeDtypeStruct((ROUTE_ROWS, T), F32)],
        compiler_params=_params(("arbitrary",)),
        name="outproj",
    )(mix_a, mix_b, w_bf, x2, g2, wr_split, br)


def _for_rows(n, fn):
    def body(r, carry):
        fn(r)
        return carry
    lax.fori_loop(0, n, body, 0, unroll=8)


def _experts_kernel(bgroup_ref, nvalid_ref, x_ref, wg_ref, wu_ref, wd_ref, y_ref):
    del bgroup_ref
    active = pl.program_id(0) < nvalid_ref[0]

    @pl.when(active)
    def _():
        rows = x_ref[...]
        x = rows[:, :D_MODEL].astype(BF16)
        ws = rows[:, D_MODEL:]
        hidden = []
        for j in range(EXPERTS_PER_GROUP):
            gate = jnp.dot(x, wg_ref[j], preferred_element_type=F32)
            up = jnp.dot(x, wu_ref[j], preferred_element_type=F32)
            hid = (jax.nn.silu(gate) * up) * ws[:, j:j + 1]
            hidden.append(hid.astype(BF16))
        hcat = jnp.concatenate(hidden, axis=1)
        wd = wd_ref[...].reshape(EXPERTS_PER_GROUP * D_FF_EXPERT, D_MODEL)
        y_ref[...] = jnp.dot(hcat, wd, preferred_element_type=F32)

    @pl.when(jnp.logical_not(active))
    def _():
        y_ref[...] = jnp.zeros(y_ref.shape, F32)


def _experts(bgroup, nvalid, rows_sorted, wg_bf, wu_bf, wd_bf):
    n_pad = rows_sorted.shape[0]
    grp = lambda i, bg, nv: (bg[i], 0, 0)
    row = lambda i, bg, nv: (i, 0)
    grid_spec = pltpu.PrefetchScalarGridSpec(
        num_scalar_prefetch=2,
        grid=(n_pad // EXPERT_BLOCK,),
        in_specs=[pl.BlockSpec((EXPERT_BLOCK, TOKEN_ROW), row),
                  pl.BlockSpec((EXPERTS_PER_GROUP, D_MODEL, D_FF_EXPERT), grp),
                  pl.BlockSpec((EXPERTS_PER_GROUP, D_MODEL, D_FF_EXPERT), grp),
                  pl.BlockSpec((EXPERTS_PER_GROUP, D_FF_EXPERT, D_MODEL), grp)],
        out_specs=pl.BlockSpec((EXPERT_BLOCK, D_MODEL), row),
    )
    return pl.pallas_call(
        _experts_kernel,
        grid_spec=grid_spec,
        out_shape=jax.ShapeDtypeStruct((n_pad, D_MODEL), F32),
        compiler_params=_params(("arbitrary",)),
        name="experts",
    )(bgroup, nvalid, rows_sorted, wg_bf, wu_bf, wd_bf)


def _dispatch_kernel(dest_ref, pads_ref, x_hbm, out_hbm, buf_ref, zero_ref, lsem, ssem, zsem):
    n = MOVE_TILE
    i = pl.program_id(0)
    steps = pl.num_programs(0)

    def load(t, s):
        return pltpu.make_async_copy(x_hbm.at[pl.ds(t * n, n)], buf_ref.at[s], lsem.at[s])

    def scatter(slot_row, r, s):
        return pltpu.make_async_copy(buf_ref.at[s, pl.ds(r, 1)], out_hbm.at[pl.ds(slot_row, 1)],
                                     ssem.at[s])

    def wait_scatters(s):
        pltpu.make_async_copy(buf_ref.at[s], out_hbm.at[pl.ds(0, n)], ssem.at[s]).wait()

    @pl.when(i == 0)
    def _():
        load(0, 0).start()
        load(1, 1).start()

    @pl.when(i >= 2)
    def _():
        wait_scatters((i + 2) % MOVE_RING)

    @pl.when(i + 2 < steps)
    def _():
        load(i + 2, (i + 2) % MOVE_RING).start()

    load(0, i % MOVE_RING).wait()

    def issue(cur):
        for r in range(n):
            scatter(dest_ref[i * n + r], r, cur).start(priority=r % 2)

    lax.switch(i % MOVE_RING, [functools.partial(issue, cur) for cur in range(MOVE_RING)])

    @pl.when(i == steps - 1)
    def _():
        wait_scatters((i + 3) % MOVE_RING)
        wait_scatters(i % MOVE_RING)
        zero_ref[...] = jnp.zeros(zero_ref.shape, F32)

        def fill(row):
            return pltpu.make_async_copy(zero_ref.at[pl.ds(0, 1)], out_hbm.at[pl.ds(row, 1)], zsem)

        def start_fill(row, carry):
            fill(row).start()
            return carry

        def wait_fill(row, carry):
            fill(row).wait()
            return carry

        for phase in (start_fill, wait_fill):
            for k in range(N_GROUPS + 1):
                lax.fori_loop(pads_ref[2 * k], pads_ref[2 * k + 1], phase, 0)


def _dispatch(dest, pads, rows, n_pad):
    T = rows.shape[0]
    assert T // MOVE_TILE >= MOVE_RING
    grid_spec = pltpu.PrefetchScalarGridSpec(
        num_scalar_prefetch=2,
        grid=(T // MOVE_TILE,),
        in_specs=[pl.BlockSpec(memory_space=pl.ANY)],
        out_specs=pl.BlockSpec(memory_space=pl.ANY),
        scratch_shapes=[pltpu.VMEM((MOVE_RING, MOVE_TILE, TOKEN_ROW), F32),
                        pltpu.VMEM((SUBLANES, TOKEN_ROW), F32),
                        pltpu.SemaphoreType.DMA((MOVE_RING,)),
                        pltpu.SemaphoreType.DMA((MOVE_RING,)),
                        pltpu.SemaphoreType.DMA(())],
    )
    return pl.pallas_call(
        _dispatch_kernel,
        grid_spec=grid_spec,
        out_shape=jax.ShapeDtypeStruct((n_pad, TOKEN_ROW), F32),
        compiler_params=_params(("arbitrary",)),
        name="dispatch",
    )(dest, pads, rows)


def _final_kernel(dest_ref, y_hbm, h_ref, g_ref, o_ref, ybuf_ref, gsem):
    n = MOVE_TILE
    i = pl.program_id(0)
    last = pl.num_programs(0) - 1

    def gather(slot_row, r, s):
        return pltpu.make_async_copy(y_hbm.at[pl.ds(slot_row, 1)], ybuf_ref.at[s, pl.ds(r, 1)],
                                     gsem.at[s])

    @pl.when(i == 0)
    def _():
        _for_rows(n, lambda r: gather(dest_ref[r], r, 0).start())
        _for_rows(n, lambda r: gather(dest_ref[n + r], r, 1).start())

    nxt = jnp.minimum(i + 2, last) * n

    def issue(cur):
        for r in range(n):
            gather(dest_ref[nxt + r], r, (cur + 2) % RING).start(priority=r % 2)

    lax.switch(i % RING, [functools.partial(issue, cur) for cur in range(RING)])

    def wait_gathers(s):
        pltpu.make_async_copy(y_hbm.at[pl.ds(0, n)], ybuf_ref.at[s], gsem.at[s]).wait()

    wait_gathers(i % RING)
    o_ref[...] = _rms(h_ref[...] + ybuf_ref[i % RING], g_ref[...])

    @pl.when(i == last)
    def _():
        wait_gathers((i + 1) % RING)
        wait_gathers((i + 2) % RING)


def _final(dest, y_sorted, h1, g):
    T = h1.shape[0]
    assert T // MOVE_TILE >= RING
    row = lambda i, dest: (i, 0)
    grid_spec = pltpu.PrefetchScalarGridSpec(
        num_scalar_prefetch=1,
        grid=(T // MOVE_TILE,),
        in_specs=[pl.BlockSpec(memory_space=pl.ANY),
                  pl.BlockSpec((MOVE_TILE, D_MODEL), row),
                  pl.BlockSpec((1, D_MODEL), lambda i, dest: (0, 0))],
        out_specs=pl.BlockSpec((MOVE_TILE, D_MODEL), row),
        scratch_shapes=[pltpu.VMEM((RING, MOVE_TILE, D_MODEL), F32),
                        pltpu.SemaphoreType.DMA((RING,))],
    )
    return pl.pallas_call(
        _final_kernel,
        grid_spec=grid_spec,
        out_shape=jax.ShapeDtypeStruct((T, D_MODEL), F32),
        compiler_params=_params(("arbitrary",)),
        name="final",
    )(dest, y_sorted, h1, g)


def _block_diag(w):
    h, d, _ = w.shape
    eye = jnp.eye(h, dtype=w.dtype)
    return (eye[:, None, :, None] * w[:, :, None, :]).reshape(h * d, h * d)


def _bias_table(rel_bias):
    rb = rel_bias.astype(F32)
    n_far = PAD_ROWS - MAX_REL + CHUNK
    far = jnp.broadcast_to(rb[:, 2 * MAX_REL:], (ATT_HEADS, n_far))
    near = rb[:, MAX_REL - CHUNK + 1:2 * MAX_REL][:, ::-1]
    diag = jnp.concatenate([far, near], axis=1)
    bias = jnp.stack([diag[:, CHUNK - 1 - q:CHUNK - 1 - q + BAND] for q in range(CHUNK)], axis=1)
    off = jnp.full((ATT_HEADS, CHUNK, CHUNK), -1e30, F32)
    first = jnp.concatenate([bias, off], axis=2)
    second = jnp.concatenate([off, bias], axis=2)
    tiles = jnp.concatenate([first, second], axis=1)
    return tiles.reshape(HEAD_GROUPS, GROUP_HEADS * Q_TILE, WINDOW)


def _dispatch_plan(grp, T):
    onehot = (grp[:, None] == jnp.arange(N_GROUPS)[None, :]).astype(jnp.int32)
    incl = jnp.cumsum(onehot, axis=0)
    counts = incl[-1]
    padded = (counts + EXPERT_BLOCK - 1) // EXPERT_BLOCK * EXPERT_BLOCK
    pend = jnp.cumsum(padded)
    pstart = pend - padded
    dest = jnp.sum((incl - onehot + pstart[None, :]) * onehot, axis=1).astype(jnp.int32)
    n_pad = T + N_GROUPS * EXPERT_BLOCK
    n_blocks = n_pad // EXPERT_BLOCK
    used_end = pstart + counts
    pads = jnp.stack([jnp.concatenate([used_end, pend[-1:]]),
                      jnp.concatenate([pend, jnp.full((1,), n_pad, pend.dtype)])], axis=1)
    pads = pads.reshape(-1).astype(jnp.int32)
    bstart = jnp.arange(n_blocks, dtype=jnp.int32) * EXPERT_BLOCK
    bgroup = jnp.sum((pend[None, :] <= bstart[:, None]).astype(jnp.int32), axis=1)
    bgroup = jnp.minimum(bgroup, N_GROUPS - 1)
    nvalid = (pend[-1:] // EXPERT_BLOCK).astype(jnp.int32)
    return dest, pads, bgroup, nvalid, n_pad


def kernel(x, norm1_g, w_in, conv_w, conv_b, w_rg_a, b_rg_a, w_rg_x, b_rg_x, lru_lambda,
           rel_bias, g_rec_out, g_att_out, w_out, norm2_g, w_group, b_group, w_router,
           b_router, w_e_gate, w_e_up, w_e_down, final_g):
    B, S, D = x.shape
    T = B * S
    assert w_in.shape[0] == 1, "single-layer block"
    (norm1_g, w_in, conv_w, conv_b, w_rg_a, b_rg_a, w_rg_x, b_rg_x, lru_lambda, rel_bias, g_rec_out,
     g_att_out, w_out, norm2_g, w_group, b_group, w_router, b_router, w_e_gate, w_e_up, w_e_down) = (
        a.reshape(a.shape[1:]) for a in (
            norm1_g, w_in, conv_w, conv_b, w_rg_a, b_rg_a, w_rg_x, b_rg_x, lru_lambda, rel_bias,
            g_rec_out, g_att_out, w_out, norm2_g, w_group, b_group, w_router, b_router,
            w_e_gate, w_e_up, w_e_down))
    h = x.reshape(T, D)
    n_exp, _, d_ff = w_e_gate.shape
    u, gate, q, k, v, wd_bf, wo_bf = _inproj(
        h, norm1_g.reshape(1, D), w_in, [w_e_down.reshape(n_exp * d_ff, D), w_out])

    wab = jnp.concatenate([_block_diag(w_rg_a), _block_diag(w_rg_x)], axis=1).astype(BF16)
    bab = jnp.concatenate([b_rg_a.reshape(1, REC_WIDTH), b_rg_x.reshape(1, REC_WIDTH)], axis=1)
    mix_a, wg_bf, wu_bf = _rglru(
        u.reshape(B, S, REC_WIDTH), gate.reshape(B, S, REC_WIDTH),
        conv_w, conv_b.reshape(1, REC_WIDTH), wab, bab,
        lru_lambda.reshape(1, REC_WIDTH), g_rec_out.reshape(1, REC_WIDTH),
        [w_e_gate.reshape(n_exp * D, d_ff), w_e_up.reshape(n_exp * D, d_ff)])

    mix_b = _attention(q.reshape(B, S, ATT_WIDTH), k.reshape(B, S, ATT_WIDTH),
                       v.reshape(B, S, ATT_WIDTH), _bias_table(rel_bias),
                       g_att_out.reshape(1, ATT_WIDTH))

    n_route = N_GROUPS + N_GROUPS * EXPERTS_PER_GROUP
    wr = jnp.concatenate([w_group.astype(F32), w_router.astype(F32)], axis=1)
    wr = jnp.pad(wr, ((0, 0), (0, ROUTE_LANES - n_route)))
    wr_hi = wr.astype(BF16)
    wr_lo = (wr - wr_hi.astype(F32)).astype(BF16)
    br = jnp.concatenate([b_group.astype(F32), b_router.astype(F32)])
    br = jnp.pad(br, (0, ROUTE_LANES - n_route)).reshape(1, ROUTE_LANES)
    h1, xn, route = _outproj(mix_a.reshape(T, REC_WIDTH), mix_b.reshape(T, ATT_WIDTH),
                             wo_bf, h, norm2_g.reshape(1, D),
                             jnp.concatenate([wr_hi, wr_lo], axis=1), br)

    grp = route[EXPERTS_PER_GROUP].astype(jnp.int32)
    dest, pads, bgroup, nvalid, n_pad = _dispatch_plan(grp, T)
    rows_sorted = _dispatch(dest, pads, xn, n_pad)
    y_sorted = _experts(bgroup, nvalid, rows_sorted, wg_bf.reshape(n_exp, D, d_ff),
                        wu_bf.reshape(n_exp, D, d_ff), wd_bf.reshape(n_exp, d_ff, D))
    out = _final(dest, y_sorted, h1, final_g.reshape(1, D))
    return out.reshape(B, S, D)
```

```python
import functools

import jax
import jax.numpy as jnp
from jax import lax
from jax.experimental import pallas as pl
from jax.experimental.pallas import tpu as pltpu

F32 = jnp.float32
BF16 = jnp.bfloat16

D_MODEL = 1024
CHUNK = 64
EPS = 1e-6
REC_WIDTH = 512
CONV_WIDTH = 4
LRU_C = 8.0
ATT_HEAD_DIM = 64
ATT_HEADS = 8
ATT_WIDTH = 512
LEFT_CHUNKS = 8
BAND = (LEFT_CHUNKS + 1) * CHUNK
MAX_REL = 128
IN_WIDTH = 2 * REC_WIDTH + 3 * ATT_WIDTH
N_GROUPS = 4
EXPERTS_PER_GROUP = 4
D_FF_EXPERT = 512

SUBLANES = 8
LANES = 128
VMEM_LIMIT_BYTES = 56 * 1024 * 1024

ROW_TILE = 1024
WEIGHT_SLAB = 128
OUT_TILE = 1024
SCAN_TILE = 1024
EXPERT_BLOCK = 256
MOVE_TILE = 512
RING = 3
MOVE_RING = 4
Q_TILE = 2 * CHUNK
WINDOW = BAND + CHUNK
GROUP_HEADS = 4
GROUP_LANES = GROUP_HEADS * ATT_HEAD_DIM
HEAD_GROUPS = ATT_HEADS // GROUP_HEADS
PAD_ROWS = LEFT_CHUNKS * CHUNK
ROUTE_LANES = LANES
ROUTE_ROWS = SUBLANES
TOKEN_ROW = D_MODEL + ROUTE_LANES


def _params(semantics):
    return pltpu.CompilerParams(dimension_semantics=semantics, vmem_limit_bytes=VMEM_LIMIT_BYTES)


def _rms(x, g):
    ms = jnp.mean(x * x, axis=-1, keepdims=True)
    return (x * lax.rsqrt(ms + EPS)) * g


def _inproj_kernel(x_ref, g_ref, w_ref, *rest):
    n = (len(rest) - 8) // 2
    riders_in, outs, riders_out = rest[:n], rest[n:n + 5], rest[n + 5:2 * n + 5]
    wbf_ref, stage_ref, wsem = rest[-3:]
    u_ref, gate_ref, q_ref, k_ref, v_ref = outs

    @pl.when(pl.program_id(0) == 0)
    def _():
        rows = stage_ref.shape[0]
        for c in range(D_MODEL // rows):
            copy = pltpu.make_async_copy(w_ref.at[pl.ds(c * rows, rows)], stage_ref, wsem)
            copy.start()
            copy.wait()
            wbf_ref[c * rows:(c + 1) * rows, :] = stage_ref[...].astype(BF16)

    xn = _rms(x_ref[...], g_ref[...])
    z = jnp.dot(xn.astype(BF16), wbf_ref[...], preferred_element_type=F32)
    u_ref[...] = z[:, :REC_WIDTH]
    gate_ref[...] = z[:, REC_WIDTH:2 * REC_WIDTH]
    o = 2 * REC_WIDTH
    q_ref[...] = (z[:, o:o + ATT_WIDTH] * (ATT_HEAD_DIM ** -0.5)).astype(BF16)
    k_ref[...] = z[:, o + ATT_WIDTH:o + 2 * ATT_WIDTH].astype(BF16)
    v_ref[...] = z[:, o + 2 * ATT_WIDTH:o + 3 * ATT_WIDTH].astype(BF16)
    for src, dst in zip(riders_in, riders_out):
        dst[...] = src[...].astype(BF16)


def _rider_specs(arrays, n_steps, index_map):
    specs = [pl.BlockSpec((a.shape[0] // n_steps, a.shape[1]), index_map) for a in arrays]
    shapes = [jax.ShapeDtypeStruct(a.shape, BF16) for a in arrays]
    return specs, shapes


def _inproj(x2, g, w, riders):
    T = x2.shape[0]
    n_steps = T // ROW_TILE
    row = lambda i: (i, 0)
    const = lambda i: (0, 0)
    half = pl.BlockSpec((ROW_TILE, REC_WIDTH), row)
    rider_specs, rider_shapes = _rider_specs(riders, n_steps, row)
    return pl.pallas_call(
        _inproj_kernel,
        grid=(n_steps,),
        in_specs=[pl.BlockSpec((ROW_TILE, D_MODEL), row),
                  pl.BlockSpec((1, D_MODEL), const),
                  pl.BlockSpec(memory_space=pl.ANY)] + rider_specs,
        out_specs=[half, half, half, half, half] + rider_specs,
        out_shape=[jax.ShapeDtypeStruct((T, REC_WIDTH), F32),
                   jax.ShapeDtypeStruct((T, REC_WIDTH), F32),
                   jax.ShapeDtypeStruct((T, ATT_WIDTH), BF16),
                   jax.ShapeDtypeStruct((T, ATT_WIDTH), BF16),
                   jax.ShapeDtypeStruct((T, ATT_WIDTH), BF16)] + rider_shapes,
        scratch_shapes=[pltpu.VMEM((D_MODEL, IN_WIDTH), BF16),
                        pltpu.VMEM((WEIGHT_SLAB, IN_WIDTH), F32),
                        pltpu.SemaphoreType.DMA(())],
        compiler_params=_params(("arbitrary",)),
        name="inproj",
    )(x2, g, w, *riders)


def _rglru_kernel(u_ref, gate_ref, convw_ref, convb_ref, wab_ref, bab_ref, lam_ref, gout_ref, *rest):
    n_riders = (len(rest) - 5) // 2
    riders_in, o_ref = rest[:n_riders], rest[n_riders]
    riders_out = rest[n_riders + 1:2 * n_riders + 1]
    tail_ref, a_ref, b_ref, carry_ref = rest[-4:]
    for src, dst in zip(riders_in, riders_out):
        dst[...] = src[...].astype(BF16)
    n = SCAN_TILE

    @pl.when(pl.program_id(1) == 0)
    def _():
        tail_ref[...] = jnp.zeros((SUBLANES, REC_WIDTH), F32)
        carry_ref[...] = jnp.zeros((SUBLANES, REC_WIDTH), F32)

    groups = n // SUBLANES
    shape3 = (groups, SUBLANES, REC_WIDTH)
    row8 = lax.broadcasted_iota(jnp.int32, shape3, 1)
    u3 = u_ref[...].reshape(shape3)
    ext3 = jnp.concatenate([tail_ref[...].reshape(1, SUBLANES, REC_WIDTH), u3], axis=0)
    tail_ref[...] = u3[groups - 1]
    uc3 = convb_ref[...].reshape(1, 1, REC_WIDTH)
    for j in range(CONV_WIDTH):
        shift = CONV_WIDTH - 1 - j
        if shift:
            rolled = pltpu.roll(ext3, shift, 1)
            x = jnp.where(row8 >= shift, rolled[1:], rolled[:-1])
        else:
            x = u3
        uc3 = uc3 + x * convw_ref[j:j + 1, :].reshape(1, 1, REC_WIDTH)
    uc = uc3.reshape(n, REC_WIDTH)

    ri = jnp.dot(uc.astype(BF16), wab_ref[...], preferred_element_type=F32) + bab_ref[...]
    r = jax.nn.sigmoid(ri[:, :REC_WIDTH])
    ig = jax.nn.sigmoid(ri[:, REC_WIDTH:])
    log_a = (-LRU_C * r) * jax.nn.softplus(-lam_ref[...])
    a = jnp.exp(log_a)
    bx = jnp.sqrt(-jnp.tanh(log_a) * (a * a + 1.0)) * (ig * uc)

    a = a.reshape(shape3)
    bx = bx.reshape(shape3)
    for s in (1, 2, 4):
        keep = row8 >= s
        a_prev = jnp.where(keep, pltpu.roll(a, s, 1), 1.0)
        b_prev = jnp.where(keep, pltpu.roll(bx, s, 1), 0.0)
        bx = a * b_prev + bx
        a = a * a_prev
    a_ref[...] = a.reshape(n, REC_WIDTH)
    b_ref[...] = bx.reshape(n, REC_WIDTH)

    def group(gi, carry):
        r0 = pl.multiple_of(gi * SUBLANES, SUBLANES)
        h = b_ref[pl.ds(r0, SUBLANES), :] + a_ref[pl.ds(r0, SUBLANES), :] * carry
        b_ref[pl.ds(r0, SUBLANES), :] = h
        return jnp.broadcast_to(h[SUBLANES - 1:SUBLANES, :], (SUBLANES, REC_WIDTH))

    carry_ref[...] = lax.fori_loop(0, n // SUBLANES, group, carry_ref[...], unroll=4)

    y = b_ref[...] * jax.nn.gelu(gate_ref[...])
    o_ref[...] = _rms(y, gout_ref[...]).astype(BF16)


def _rglru(u3, gate3, conv_w, conv_b, wab_bf, bab, lam, g_out, riders):
    B, S, _ = u3.shape
    n_tiles = S // SCAN_TILE
    tile = pl.BlockSpec((None, SCAN_TILE, REC_WIDTH), lambda b, i: (b, i, 0))
    const = lambda b, i: (0, 0)
    vec = pl.BlockSpec((1, REC_WIDTH), const)
    rider_specs, rider_shapes = _rider_specs(riders, B * n_tiles, lambda b, i: (b * n_tiles + i, 0))
    return pl.pallas_call(
        _rglru_kernel,
        grid=(B, n_tiles),
        in_specs=[tile, tile,
                  pl.BlockSpec((CONV_WIDTH, REC_WIDTH), const), vec,
                  pl.BlockSpec((REC_WIDTH, 2 * REC_WIDTH), const),
                  pl.BlockSpec((1, 2 * REC_WIDTH), const), vec, vec] + rider_specs,
        out_specs=[tile] + rider_specs,
        out_shape=[jax.ShapeDtypeStruct((B, S, REC_WIDTH), BF16)] + rider_shapes,
        scratch_shapes=[pltpu.VMEM((SUBLANES, REC_WIDTH), F32),
                        pltpu.VMEM((SCAN_TILE, REC_WIDTH), F32),
                        pltpu.VMEM((SCAN_TILE, REC_WIDTH), F32),
                        pltpu.VMEM((SUBLANES, REC_WIDTH), F32)],
        compiler_params=_params(("arbitrary", "arbitrary")),
        name="rglru",
    )(u3, gate3, conv_w, conv_b, wab_bf, bab, lam, g_out, *riders)


def _attn_kernel(q_ref, k_ref, v_ref, bias_ref, gout_ref, o_ref):
    S = q_ref.shape[0]
    lane_head = lax.broadcasted_iota(jnp.int32, (Q_TILE, GROUP_LANES), 1) // ATT_HEAD_DIM

    def tile(t, skip):
        r0 = t * Q_TILE
        k0 = r0 - PAD_ROWS + skip
        if not isinstance(t, int):
            r0 = pl.multiple_of(r0, Q_TILE)
            k0 = pl.multiple_of(k0, Q_TILE)
        q_rows = q_ref[pl.ds(r0, Q_TILE), :]
        k_win = k_ref[pl.ds(k0, WINDOW - skip), :]
        v_win = v_ref[pl.ds(k0, WINDOW - skip), :]
        outs = []
        for hg in range(HEAD_GROUPS):
            sl = slice(hg * GROUP_LANES, (hg + 1) * GROUP_LANES)
            q4 = q_rows[:, sl]
            zero = jnp.zeros_like(q4)
            q_stack = jnp.concatenate(
                [jnp.where(lane_head == j, q4, zero) for j in range(GROUP_HEADS)], axis=0)
            s = lax.dot_general(q_stack, k_win[:, sl], (((1,), (1,)), ((), ())),
                                preferred_element_type=F32)
            s = s + bias_ref[hg][:, skip:]
            m = jnp.max(s, axis=-1, keepdims=True)
            e = jnp.exp(s - m)
            denom = jnp.sum(e, axis=-1, keepdims=True)
            o = jnp.dot(e.astype(BF16), v_win[:, sl], preferred_element_type=F32)
            o = o / denom
            out = o[:Q_TILE]
            for j in range(1, GROUP_HEADS):
                out = jnp.where(lane_head == j, o[j * Q_TILE:(j + 1) * Q_TILE], out)
            outs.append(out)
        y = jnp.concatenate(outs, axis=1)
        o_ref[pl.ds(r0, Q_TILE), :] = _rms(y, gout_ref[...]).astype(BF16)

    first_full = PAD_ROWS // Q_TILE
    for t in range(first_full):
        tile(t, PAD_ROWS - t * Q_TILE)

    def full_body(t, carry):
        tile(t, 0)
        return carry

    lax.fori_loop(first_full, S // Q_TILE, full_body, 0, unroll=6)


def _attention(q3, k3, v3, bias_tiles, g_out):
    B, S, _ = q3.shape
    seq = pl.BlockSpec((None, S, ATT_WIDTH), lambda b: (b, 0, 0))
    return pl.pallas_call(
        _attn_kernel,
        grid=(B,),
        in_specs=[seq, seq, seq,
                  pl.BlockSpec((HEAD_GROUPS, GROUP_HEADS * Q_TILE, WINDOW), lambda b: (0, 0, 0)),
                  pl.BlockSpec((1, ATT_WIDTH), lambda b: (0, 0))],
        out_specs=seq,
        out_shape=jax.ShapeDtypeStruct((B, S, ATT_WIDTH), BF16),
        compiler_params=_params(("arbitrary",)),
        name="attn",
    )(q3, k3, v3, bias_tiles, g_out)


def _route(lt):
    n_tok = lt.shape[1]
    row = lambda j: lt[j:j + 1, :]

    gl = [row(j) for j in range(N_GROUPS)]
    gmax = jnp.maximum(jnp.maximum(gl[0], gl[1]), jnp.maximum(gl[2], gl[3]))
    grp = jnp.where(gl[0] >= gmax, 0, jnp.where(gl[1] >= gmax, 1, jnp.where(gl[2] >= gmax, 2, 3)))
    denom = (jnp.exp(gl[0] - gmax) + jnp.exp(gl[1] - gmax)
             + jnp.exp(gl[2] - gmax) + jnp.exp(gl[3] - gmax))
    p_g = 1.0 / denom

    sel = []
    for j in range(EXPERTS_PER_GROUP):
        cand = [row(N_GROUPS + g * EXPERTS_PER_GROUP + j) for g in range(N_GROUPS)]
        sel.append(jnp.where(grp == 0, cand[0],
                             jnp.where(grp == 1, cand[1], jnp.where(grp == 2, cand[2], cand[3]))))

    def first_argmax(vals):
        top = jnp.maximum(jnp.maximum(vals[0], vals[1]), jnp.maximum(vals[2], vals[3]))
        idx = jnp.where(vals[0] >= top, 0,
                        jnp.where(vals[1] >= top, 1, jnp.where(vals[2] >= top, 2, 3)))
        return top, idx

    v1, i1 = first_argmax(sel)
    rest = [jnp.where(i1 == j, -jnp.inf, sel[j]) for j in range(EXPERTS_PER_GROUP)]
    v2, i2 = first_argmax(rest)
    t = jnp.exp(v2 - v1)
    w1 = p_g * (1.0 / (1.0 + t))
    w2 = p_g * (t / (1.0 + t))
    sub = lax.broadcasted_iota(jnp.int32, (ROUTE_ROWS, n_tok), 0)
    table = jnp.zeros((ROUTE_ROWS, n_tok), F32)
    for j in range(EXPERTS_PER_GROUP):
        w_j = jnp.where(i1 == j, w1, 0.0) + jnp.where(i2 == j, w2, 0.0)
        table = jnp.where(sub == j, jnp.broadcast_to(w_j, (ROUTE_ROWS, n_tok)), table)
    return table, grp


def _outproj_kernel(ma_ref, mb_ref, w_ref, x_ref, g2_ref, wr_ref, br_ref,
                    h_ref, xn_ref, route_ref):
    mix = jnp.concatenate([ma_ref[...], mb_ref[...]], axis=1)
    h = x_ref[...] + jnp.dot(mix, w_ref[...], preferred_element_type=F32)
    h_ref[...] = h
    xn = _rms(h, g2_ref[...])
    xn_ref[:, :D_MODEL] = xn

    x_hi = xn.astype(BF16)
    x_lo = (xn - x_hi.astype(F32)).astype(BF16)
    wr = wr_ref[...]
    both = jnp.dot(x_hi, wr, preferred_element_type=F32)
    logits = (both[:, :ROUTE_LANES] + both[:, ROUTE_LANES:]
              + jnp.dot(x_lo, wr[:, :ROUTE_LANES], preferred_element_type=F32)
              + br_ref[...])
    table, grp = _route(logits.T)
    n_tok = route_ref.shape[1]
    sub = lax.broadcasted_iota(jnp.int32, (ROUTE_ROWS, n_tok), 0)
    route_ref[...] = jnp.where(sub == EXPERTS_PER_GROUP,
                               jnp.broadcast_to(grp.astype(F32), (ROUTE_ROWS, n_tok)), table)
    padded = jnp.concatenate([table, jnp.zeros((ROUTE_LANES - ROUTE_ROWS, n_tok), F32)], axis=0)
    xn_ref[:, D_MODEL:] = padded.T


def _outproj(mix_a, mix_b, w_bf, x2, g2, wr_split, br):
    T = x2.shape[0]
    row = lambda i: (i, 0)
    const = lambda i: (0, 0)
    full = pl.BlockSpec((OUT_TILE, D_MODEL), row)
    half = pl.BlockSpec((OUT_TILE, REC_WIDTH), row)
    return pl.pallas_call(
        _outproj_kernel,
        grid=(T // OUT_TILE,),
        in_specs=[half, half,
                  pl.BlockSpec((D_MODEL, D_MODEL), const),
                  full,
                  pl.BlockSpec((1, D_MODEL), const),
                  pl.BlockSpec((D_MODEL, 2 * ROUTE_LANES), const),
                  pl.BlockSpec((1, ROUTE_LANES), const)],
        out_specs=[full, pl.BlockSpec((OUT_TILE, TOKEN_ROW), row),
                   pl.BlockSpec((ROUTE_ROWS, OUT_TILE), lambda i: (0, i))],
        out_shape=[jax.ShapeDtypeStruct((T, D_MODEL), F32),
                   jax.ShapeDtypeStruct((T, TOKEN_ROW), F32),
                   jax.ShapeDtypeStruct((ROUTE_ROWS, T), F32)],
        compiler_params=_params(("arbitrary",)),
        name="outproj",
    )(mix_a, mix_b, w_bf, x2, g2, wr_split, br)


def _for_rows(n, fn):
    def body(r, carry):
        fn(r)
        return carry
    lax.fori_loop(0, n, body, 0, unroll=8)


def _experts_kernel(bgroup_ref, nvalid_ref, x_ref, wg_ref, wu_ref, wd_ref, y_ref):
    del bgroup_ref
    active = pl.program_id(0) < nvalid_ref[0]

    @pl.when(active)
    def _():
        rows = x_ref[...]
        x = rows[:, :D_MODEL].astype(BF16)
        ws = rows[:, D_MODEL:]
        hidden = []
        for j in range(EXPERTS_PER_GROUP):
            gate = jnp.dot(x, wg_ref[j], preferred_element_type=F32)
            up = jnp.dot(x, wu_ref[j], preferred_element_type=F32)
            hid = (jax.nn.silu(gate) * up) * ws[:, j:j + 1]
            hidden.append(hid.astype(BF16))
        hcat = jnp.concatenate(hidden, axis=1)
        wd = wd_ref[...].reshape(EXPERTS_PER_GROUP * D_FF_EXPERT, D_MODEL)
        y_ref[...] = jnp.dot(hcat, wd, preferred_element_type=F32)

    @pl.when(jnp.logical_not(active))
    def _():
        y_ref[...] = jnp.zeros(y_ref.shape, F32)


def _experts(bgroup, nvalid, rows_sorted, wg_bf, wu_bf, wd_bf):
    n_pad = rows_sorted.shape[0]
    grp = lambda i, bg, nv: (bg[i], 0, 0)
    row = lambda i, bg, nv: (i, 0)
    grid_spec = pltpu.PrefetchScalarGridSpec(
        num_scalar_prefetch=2,
        grid=(n_pad // EXPERT_BLOCK,),
        in_specs=[pl.BlockSpec((EXPERT_BLOCK, TOKEN_ROW), row),
                  pl.BlockSpec((EXPERTS_PER_GROUP, D_MODEL, D_FF_EXPERT), grp),
                  pl.BlockSpec((EXPERTS_PER_GROUP, D_MODEL, D_FF_EXPERT), grp),
                  pl.BlockSpec((EXPERTS_PER_GROUP, D_FF_EXPERT, D_MODEL), grp)],
        out_specs=pl.BlockSpec((EXPERT_BLOCK, D_MODEL), row),
    )
    return pl.pallas_call(
        _experts_kernel,
        grid_spec=grid_spec,
        out_shape=jax.ShapeDtypeStruct((n_pad, D_MODEL), F32),
        compiler_params=_params(("arbitrary",)),
        name="experts",
    )(bgroup, nvalid, rows_sorted, wg_bf, wu_bf, wd_bf)


def _dispatch_kernel(dest_ref, pads_ref, x_hbm, out_hbm, buf_ref, zero_ref, lsem, ssem, zsem):
    n = MOVE_TILE
    i = pl.program_id(0)
    steps = pl.num_programs(0)

    def load(t, s):
        return pltpu.make_async_copy(x_hbm.at[pl.ds(t * n, n)], buf_ref.at[s], lsem.at[s])

    def scatter(slot_row, r, s):
        return pltpu.make_async_copy(buf_ref.at[s, pl.ds(r, 1)], out_hbm.at[pl.ds(slot_row, 1)],
                                     ssem.at[s])

    def wait_scatters(s):
        pltpu.make_async_copy(buf_ref.at[s], out_hbm.at[pl.ds(0, n)], ssem.at[s]).wait()

    @pl.when(i == 0)
    def _():
        load(0, 0).start()
        load(1, 1).start()

    @pl.when(i >= 2)
    def _():
        wait_scatters((i + 2) % MOVE_RING)

    @pl.when(i + 2 < steps)
    def _():
        load(i + 2, (i + 2) % MOVE_RING).start()

    load(0, i % MOVE_RING).wait()

    def issue(cur):
        for r in range(n):
            scatter(dest_ref[i * n + r], r, cur).start(priority=r % 2)

    lax.switch(i % MOVE_RING, [functools.partial(issue, cur) for cur in range(MOVE_RING)])

    @pl.when(i == steps - 1)
    def _():
        wait_scatters((i + 3) % MOVE_RING)
        wait_scatters(i % MOVE_RING)
        zero_ref[...] = jnp.zeros(zero_ref.shape, F32)

        def fill(row):
            return pltpu.make_async_copy(zero_ref.at[pl.ds(0, 1)], out_hbm.at[pl.ds(row, 1)], zsem)

        def start_fill(row, carry):
            fill(row).start()
            return carry

        def wait_fill(row, carry):
            fill(row).wait()
            return carry

        for phase in (start_fill, wait_fill):
            for k in range(N_GROUPS + 1):
                lax.fori_loop(pads_ref[2 * k], pads_ref[2 * k + 1], phase, 0)


def _dispatch(dest, pads, rows, n_pad):
    T = rows.shape[0]
    assert T // MOVE_TILE >= MOVE_RING
    grid_spec = pltpu.PrefetchScalarGridSpec(
        num_scalar_prefetch=2,
        grid=(T // MOVE_TILE,),
        in_specs=[pl.BlockSpec(memory_space=pl.ANY)],
        out_specs=pl.BlockSpec(memory_space=pl.ANY),
        scratch_shapes=[pltpu.VMEM((MOVE_RING, MOVE_TILE, TOKEN_ROW), F32),
                        pltpu.VMEM((SUBLANES, TOKEN_ROW), F32),
                        pltpu.SemaphoreType.DMA((MOVE_RING,)),
                        pltpu.SemaphoreType.DMA((MOVE_RING,)),
                        pltpu.SemaphoreType.DMA(())],
    )
    return pl.pallas_call(
        _dispatch_kernel,
        grid_spec=grid_spec,
        out_shape=jax.ShapeDtypeStruct((n_pad, TOKEN_ROW), F32),
        compiler_params=_params(("arbitrary",)),
        name="dispatch",
    )(dest, pads, rows)


def _final_kernel(dest_ref, y_hbm, h_ref, g_ref, o_ref, ybuf_ref, gsem):
    n = MOVE_TILE
    i = pl.program_id(0)
    last = pl.num_programs(0) - 1

    def gather(slot_row, r, s):
        return pltpu.make_async_copy(y_hbm.at[pl.ds(slot_row, 1)], ybuf_ref.at[s, pl.ds(r, 1)],
                                     gsem.at[s])

    @pl.when(i == 0)
    def _():
        _for_rows(n, lambda r: gather(dest_ref[r], r, 0).start())
        _for_rows(n, lambda r: gather(dest_ref[n + r], r, 1).start())

    nxt = jnp.minimum(i + 2, last) * n

    def issue(cur):
        for r in range(n):
            gather(dest_ref[nxt + r], r, (cur + 2) % RING).start(priority=r % 2)

    lax.switch(i % RING, [functools.partial(issue, cur) for cur in range(RING)])

    def wait_gathers(s):
        pltpu.make_async_copy(y_hbm.at[pl.ds(0, n)], ybuf_ref.at[s], gsem.at[s]).wait()

    wait_gathers(i % RING)
    o_ref[...] = _rms(h_ref[...] + ybuf_ref[i % RING], g_ref[...])

    @pl.when(i == last)
    def _():
        wait_gathers((i + 1) % RING)
        wait_gathers((i + 2) % RING)


def _final(dest, y_sorted, h1, g):
    T = h1.shape[0]
    assert T // MOVE_TILE >= RING
    row = lambda i, dest: (i, 0)
    grid_spec = pltpu.PrefetchScalarGridSpec(
        num_scalar_prefetch=1,
        grid=(T // MOVE_TILE,),
        in_specs=[pl.BlockSpec(memory_space=pl.ANY),
                  pl.BlockSpec((MOVE_TILE, D_MODEL), row),
                  pl.BlockSpec((1, D_MODEL), lambda i, dest: (0, 0))],
        out_specs=pl.BlockSpec((MOVE_TILE, D_MODEL), row),
        scratch_shapes=[pltpu.VMEM((RING, MOVE_TILE, D_MODEL), F32),
                        pltpu.SemaphoreType.DMA((RING,))],
    )
    return pl.pallas_call(
        _final_kernel,
        grid_spec=grid_spec,
        out_shape=jax.ShapeDtypeStruct((T, D_MODEL), F32),
        compiler_params=_params(("arbitrary",)),
        name="final",
    )(dest, y_sorted, h1, g)


def _block_diag(w):
    h, d, _ = w.shape
    eye = jnp.eye(h, dtype=w.dtype)
    return (eye[:, None, :, None] * w[:, :, None, :]).reshape(h * d, h * d)


def _bias_table(rel_bias):
    rb = rel_bias.astype(F32)
    n_far = PAD_ROWS - MAX_REL + CHUNK
    far = jnp.broadcast_to(rb[:, 2 * MAX_REL:], (ATT_HEADS, n_far))
    near = rb[:, MAX_REL - CHUNK + 1:2 * MAX_REL][:, ::-1]
    diag = jnp.concatenate([far, near], axis=1)
    bias = jnp.stack([diag[:, CHUNK - 1 - q:CHUNK - 1 - q + BAND] for q in range(CHUNK)], axis=1)
    off = jnp.full((ATT_HEADS, CHUNK, CHUNK), -1e30, F32)
    first = jnp.concatenate([bias, off], axis=2)
    second = jnp.concatenate([off, bias], axis=2)
    tiles = jnp.concatenate([first, second], axis=1)
    return tiles.reshape(HEAD_GROUPS, GROUP_HEADS * Q_TILE, WINDOW)


def _dispatch_plan(grp, T):
    onehot = (grp[:, None] == jnp.arange(N_GROUPS)[None, :]).astype(jnp.int32)
    incl = jnp.cumsum(onehot, axis=0)
    counts = incl[-1]
    padded = (counts + EXPERT_BLOCK - 1) // EXPERT_BLOCK * EXPERT_BLOCK
    pend = jnp.cumsum(padded)
    pstart = pend - padded
    dest = jnp.sum((incl - onehot + pstart[None, :]) * onehot, axis=1).astype(jnp.int32)
    n_pad = T + N_GROUPS * EXPERT_BLOCK
    n_blocks = n_pad // EXPERT_BLOCK
    used_end = pstart + counts
    pads = jnp.stack([jnp.concatenate([used_end, pend[-1:]]),
                      jnp.concatenate([pend, jnp.full((1,), n_pad, pend.dtype)])], axis=1)
    pads = pads.reshape(-1).astype(jnp.int32)
    bstart = jnp.arange(n_blocks, dtype=jnp.int32) * EXPERT_BLOCK
    bgroup = jnp.sum((pend[None, :] <= bstart[:, None]).astype(jnp.int32), axis=1)
    bgroup = jnp.minimum(bgroup, N_GROUPS - 1)
    nvalid = (pend[-1:] // EXPERT_BLOCK).astype(jnp.int32)
    return dest, pads, bgroup, nvalid, n_pad


def kernel(x, norm1_g, w_in, conv_w, conv_b, w_rg_a, b_rg_a, w_rg_x, b_rg_x, lru_lambda,
           rel_bias, g_rec_out, g_att_out, w_out, norm2_g, w_group, b_group, w_router,
           b_router, w_e_gate, w_e_up, w_e_down, final_g):
    B, S, D = x.shape
    T = B * S
    assert w_in.shape[0] == 1, "single-layer block"
    (norm1_g, w_in, conv_w, conv_b, w_rg_a, b_rg_a, w_rg_x, b_rg_x, lru_lambda, rel_bias, g_rec_out,
     g_att_out, w_out, norm2_g, w_group, b_group, w_router, b_router, w_e_gate, w_e_up, w_e_down) = (
        a.reshape(a.shape[1:]) for a in (
            norm1_g, w_in, conv_w, conv_b, w_rg_a, b_rg_a, w_rg_x, b_rg_x, lru_lambda, rel_bias,
            g_rec_out, g_att_out, w_out, norm2_g, w_group, b_group, w_router, b_router,
            w_e_gate, w_e_up, w_e_down))
    h = x.reshape(T, D)
    n_exp, _, d_ff = w_e_gate.shape
    u, gate, q, k, v, wd_bf, wo_bf = _inproj(
        h, norm1_g.reshape(1, D), w_in, [w_e_down.reshape(n_exp * d_ff, D), w_out])

    wab = jnp.concatenate([_block_diag(w_rg_a), _block_diag(w_rg_x)], axis=1).astype(BF16)
    bab = jnp.concatenate([b_rg_a.reshape(1, REC_WIDTH), b_rg_x.reshape(1, REC_WIDTH)], axis=1)
    mix_a, wg_bf, wu_bf = _rglru(
        u.reshape(B, S, REC_WIDTH), gate.reshape(B, S, REC_WIDTH),
        conv_w, conv_b.reshape(1, REC_WIDTH), wab, bab,
        lru_lambda.reshape(1, REC_WIDTH), g_rec_out.reshape(1, REC_WIDTH),
        [w_e_gate.reshape(n_exp * D, d_ff), w_e_up.reshape(n_exp * D, d_ff)])

    mix_b = _attention(q.reshape(B, S, ATT_WIDTH), k.reshape(B, S, ATT_WIDTH),
                       v.reshape(B, S, ATT_WIDTH), _bias_table(rel_bias),
                       g_att_out.reshape(1, ATT_WIDTH))

    n_route = N_GROUPS + N_GROUPS * EXPERTS_PER_GROUP
    wr = jnp.concatenate([w_group.astype(F32), w_router.astype(F32)], axis=1)
    wr = jnp.pad(wr, ((0, 0), (0, ROUTE_LANES - n_route)))
    wr_hi = wr.astype(BF16)
    wr_lo = (wr - wr_hi.astype(F32)).astype(BF16)
    br = jnp.concatenate([b_group.astype(F32), b_router.astype(F32)])
    br = jnp.pad(br, (0, ROUTE_LANES - n_route)).reshape(1, ROUTE_LANES)
    h1, xn, route = _outproj(mix_a.reshape(T, REC_WIDTH), mix_b.reshape(T, ATT_WIDTH),
                             wo_bf, h, norm2_g.reshape(1, D),
                             jnp.concatenate([wr_hi, wr_lo], axis=1), br)

    grp = route[EXPERTS_PER_GROUP].astype(jnp.int32)
    dest, pads, bgroup, nvalid, n_pad = _dispatch_plan(grp, T)
    rows_sorted = _dispatch(dest, pads, xn, n_pad)
    y_sorted = _experts(bgroup, nvalid, rows_sorted, wg_bf.reshape(n_exp, D, d_ff),
                        wu_bf.reshape(n_exp, D, d_ff), wd_bf.reshape(n_exp, d_ff, D))
    out = _final(dest, y_sorted, h1, final_g.reshape(1, D))
    return out.reshape(B, S, D)
```

```python
import functools

import jax
import jax.numpy as jnp
from jax import lax
from jax.experimental import pallas as pl
from jax.experimental.pallas import tpu as pltpu

F32 = jnp.float32
BF16 = jnp.bfloat16

D_MODEL = 1024
CHUNK = 64
EPS = 1e-6
REC_WIDTH = 512
CONV_WIDTH = 4
LRU_C = 8.0
ATT_HEAD_DIM = 64
ATT_HEADS = 8
ATT_WIDTH = 512
LEFT_CHUNKS = 8
BAND = (LEFT_CHUNKS + 1) * CHUNK
MAX_REL = 128
IN_WIDTH = 2 * REC_WIDTH + 3 * ATT_WIDTH
N_GROUPS = 4
EXPERTS_PER_GROUP = 4
D_FF_EXPERT = 512

SUBLANES = 8
LANES = 128
VMEM_LIMIT_BYTES = 56 * 1024 * 1024

ROW_TILE = 1024
WEIGHT_SLAB = 128
OUT_TILE = 1024
SCAN_TILE = 1024
EXPERT_BLOCK = 256
MOVE_TILE = 512
RING = 3
MOVE_RING = 4
Q_TILE = 2 * CHUNK
WINDOW = BAND + CHUNK
GROUP_HEADS = 4
GROUP_LANES = GROUP_HEADS * ATT_HEAD_DIM
HEAD_GROUPS = ATT_HEADS // GROUP_HEADS
PAD_ROWS = LEFT_CHUNKS * CHUNK
ROUTE_LANES = LANES
ROUTE_ROWS = SUBLANES
TOKEN_ROW = D_MODEL + ROUTE_LANES


def _params(semantics):
    return pltpu.CompilerParams(dimension_semantics=semantics, vmem_limit_bytes=VMEM_LIMIT_BYTES)


def _rms(x, g):
    ms = jnp.mean(x * x, axis=-1, keepdims=True)
    return (x * lax.rsqrt(ms + EPS)) * g


def _inproj_kernel(x_ref, g_ref, w_ref, *rest):
    n = (len(rest) - 8) // 2
    riders_in, outs, riders_out = rest[:n], rest[n:n + 5], rest[n + 5:2 * n + 5]
    wbf_ref, stage_ref, wsem = rest[-3:]
    u_ref, gate_ref, q_ref, k_ref, v_ref = outs

    @pl.when(pl.program_id(0) == 0)
    def _():
        rows = stage_ref.shape[1]
        n_slabs = D_MODEL // rows

        def slab(c):
            return pltpu.make_async_copy(w_ref.at[pl.ds(c * rows, rows)], stage_ref.at[c % 2],
                                         wsem.at[c % 2])

        slab(0).start()
        for c in range(n_slabs):
            if c + 1 < n_slabs:
                slab(c + 1).start()
            slab(c).wait()
            wbf_ref[c * rows:(c + 1) * rows, :] = stage_ref[c % 2].astype(BF16)

    xn = _rms(x_ref[...], g_ref[...])
    z = jnp.dot(xn.astype(BF16), wbf_ref[...], preferred_element_type=F32)
    u_ref[...] = z[:, :REC_WIDTH]
    gate_ref[...] = z[:, REC_WIDTH:2 * REC_WIDTH]
    o = 2 * REC_WIDTH
    q_ref[...] = (z[:, o:o + ATT_WIDTH] * (ATT_HEAD_DIM ** -0.5)).astype(BF16)
    k_ref[...] = z[:, o + ATT_WIDTH:o + 2 * ATT_WIDTH].astype(BF16)
    v_ref[...] = z[:, o + 2 * ATT_WIDTH:o + 3 * ATT_WIDTH].astype(BF16)
    for src, dst in zip(riders_in, riders_out):
        dst[...] = src[...].astype(BF16)


def _rider_specs(arrays, n_steps, index_map):
    specs = [pl.BlockSpec((a.shape[0] // n_steps, a.shape[1]), index_map) for a in arrays]
    shapes = [jax.ShapeDtypeStruct(a.shape, BF16) for a in arrays]
    return specs, shapes


def _inproj(x2, g, w, riders):
    T = x2.shape[0]
    n_steps = T // ROW_TILE
    row = lambda i: (i, 0)
    const = lambda i: (0, 0)
    half = pl.BlockSpec((ROW_TILE, REC_WIDTH), row)
    rider_specs, rider_shapes = _rider_specs(riders, n_steps, row)
    return pl.pallas_call(
        _inproj_kernel,
        grid=(n_steps,),
        in_specs=[pl.BlockSpec((ROW_TILE, D_MODEL), row),
                  pl.BlockSpec((1, D_MODEL), const),
                  pl.BlockSpec(memory_space=pl.ANY)] + rider_specs,
        out_specs=[half, half, half, half, half] + rider_specs,
        out_shape=[jax.ShapeDtypeStruct((T, REC_WIDTH), F32),
                   jax.ShapeDtypeStruct((T, REC_WIDTH), F32),
                   jax.ShapeDtypeStruct((T, ATT_WIDTH), BF16),
                   jax.ShapeDtypeStruct((T, ATT_WIDTH), BF16),
                   jax.ShapeDtypeStruct((T, ATT_WIDTH), BF16)] + rider_shapes,
        scratch_shapes=[pltpu.VMEM((D_MODEL, IN_WIDTH), BF16),
                        pltpu.VMEM((2, WEIGHT_SLAB, IN_WIDTH), F32),
                        pltpu.SemaphoreType.DMA((2,))],
        compiler_params=_params(("arbitrary",)),
        name="inproj",
    )(x2, g, w, *riders)


def _rglru_kernel(u_ref, gate_ref, convw_ref, convb_ref, wab_ref, bab_ref, lam_ref, gout_ref, *rest):
    n_riders = (len(rest) - 5) // 2
    riders_in, o_ref = rest[:n_riders], rest[n_riders]
    riders_out = rest[n_riders + 1:2 * n_riders + 1]
    tail_ref, a_ref, b_ref, carry_ref = rest[-4:]
    for src, dst in zip(riders_in, riders_out):
        dst[...] = src[...].astype(BF16)
    n = SCAN_TILE

    @pl.when(pl.program_id(1) == 0)
    def _():
        tail_ref[...] = jnp.zeros((SUBLANES, REC_WIDTH), F32)
        carry_ref[...] = jnp.zeros((SUBLANES, REC_WIDTH), F32)

    groups = n // SUBLANES
    shape3 = (groups, SUBLANES, REC_WIDTH)
    row8 = lax.broadcasted_iota(jnp.int32, shape3, 1)
    u3 = u_ref[...].reshape(shape3)
    ext3 = jnp.concatenate([tail_ref[...].reshape(1, SUBLANES, REC_WIDTH), u3], axis=0)
    tail_ref[...] = u3[groups - 1]
    uc3 = convb_ref[...].reshape(1, 1, REC_WIDTH)
    for j in range(CONV_WIDTH):
        shift = CONV_WIDTH - 1 - j
        if shift:
            rolled = pltpu.roll(ext3, shift, 1)
            x = jnp.where(row8 >= shift, rolled[1:], rolled[:-1])
        else:
            x = u3
        uc3 = uc3 + x * convw_ref[j:j + 1, :].reshape(1, 1, REC_WIDTH)
    uc = uc3.reshape(n, REC_WIDTH)

    ri = jnp.dot(uc.astype(BF16), wab_ref[...], preferred_element_type=F32) + bab_ref[...]
    r = jax.nn.sigmoid(ri[:, :REC_WIDTH])
    ig = jax.nn.sigmoid(ri[:, REC_WIDTH:])
    log_a = (-LRU_C * r) * jax.nn.softplus(-lam_ref[...])
    a = jnp.exp(log_a)
    bx = jnp.sqrt(-jnp.tanh(log_a) * (a * a + 1.0)) * (ig * uc)

    a = a.reshape(shape3)
    bx = bx.reshape(shape3)
    for s in (1, 2, 4):
        keep = row8 >= s
        a_prev = jnp.where(keep, pltpu.roll(a, s, 1), 1.0)
        b_prev = jnp.where(keep, pltpu.roll(bx, s, 1), 0.0)
        bx = a * b_prev + bx
        a = a * a_prev
    a_ref[...] = a.reshape(n, REC_WIDTH)
    b_ref[...] = bx.reshape(n, REC_WIDTH)

    def group(gi, carry):
        r0 = pl.multiple_of(gi * SUBLANES, SUBLANES)
        h = b_ref[pl.ds(r0, SUBLANES), :] + a_ref[pl.ds(r0, SUBLANES), :] * carry
        b_ref[pl.ds(r0, SUBLANES), :] = h
        return jnp.broadcast_to(h[SUBLANES - 1:SUBLANES, :], (SUBLANES, REC_WIDTH))

    carry_ref[...] = lax.fori_loop(0, n // SUBLANES, group, carry_ref[...], unroll=4)

    y = b_ref[...] * jax.nn.gelu(gate_ref[...])
    o_ref[...] = _rms(y, gout_ref[...]).astype(BF16)


def _rglru(u3, gate3, conv_w, conv_b, wab_bf, bab, lam, g_out, riders):
    B, S, _ = u3.shape
    n_tiles = S // SCAN_TILE
    tile = pl.BlockSpec((None, SCAN_TILE, REC_WIDTH), lambda b, i: (b, i, 0))
    const = lambda b, i: (0, 0)
    vec = pl.BlockSpec((1, REC_WIDTH), const)
    rider_specs, rider_shapes = _rider_specs(riders, B * n_tiles, lambda b, i: (b * n_tiles + i, 0))
    return pl.pallas_call(
        _rglru_kernel,
        grid=(B, n_tiles),
        in_specs=[tile, tile,
                  pl.BlockSpec((CONV_WIDTH, REC_WIDTH), const), vec,
                  pl.BlockSpec((REC_WIDTH, 2 * REC_WIDTH), const),
                  pl.BlockSpec((1, 2 * REC_WIDTH), const), vec, vec] + rider_specs,
        out_specs=[tile] + rider_specs,
        out_shape=[jax.ShapeDtypeStruct((B, S, REC_WIDTH), BF16)] + rider_shapes,
        scratch_shapes=[pltpu.VMEM((SUBLANES, REC_WIDTH), F32),
                        pltpu.VMEM((SCAN_TILE, REC_WIDTH), F32),
                        pltpu.VMEM((SCAN_TILE, REC_WIDTH), F32),
                        pltpu.VMEM((SUBLANES, REC_WIDTH), F32)],
        compiler_params=_params(("arbitrary", "arbitrary")),
        name="rglru",
    )(u3, gate3, conv_w, conv_b, wab_bf, bab, lam, g_out, *riders)


def _attn_kernel(q_ref, k_ref, v_ref, bias_ref, gout_ref, o_ref):
    S = q_ref.shape[0]
    lane_head = lax.broadcasted_iota(jnp.int32, (Q_TILE, GROUP_LANES), 1) // ATT_HEAD_DIM

    def tile(t, skip):
        r0 = t * Q_TILE
        k0 = r0 - PAD_ROWS + skip
        if not isinstance(t, int):
            r0 = pl.multiple_of(r0, Q_TILE)
            k0 = pl.multiple_of(k0, Q_TILE)
        q_rows = q_ref[pl.ds(r0, Q_TILE), :]
        k_win = k_ref[pl.ds(k0, WINDOW - skip), :]
        v_win = v_ref[pl.ds(k0, WINDOW - skip), :]
        outs = []
        for hg in range(HEAD_GROUPS):
            sl = slice(hg * GROUP_LANES, (hg + 1) * GROUP_LANES)
            q4 = q_rows[:, sl]
            zero = jnp.zeros_like(q4)
            q_stack = jnp.concatenate(
                [jnp.where(lane_head == j, q4, zero) for j in range(GROUP_HEADS)], axis=0)
            s = lax.dot_general(q_stack, k_win[:, sl], (((1,), (1,)), ((), ())),
                                preferred_element_type=F32)
            s = s + bias_ref[hg][:, skip:]
            m = jnp.max(s, axis=-1, keepdims=True)
            e = jnp.exp(s - m)
            denom = jnp.sum(e, axis=-1, keepdims=True)
            o = jnp.dot(e.astype(BF16), v_win[:, sl], preferred_element_type=F32)
            o = o / denom
            out = o[:Q_TILE]
            for j in range(1, GROUP_HEADS):
                out = jnp.where(lane_head == j, o[j * Q_TILE:(j + 1) * Q_TILE], out)
            outs.append(out)
        y = jnp.concatenate(outs, axis=1)
        o_ref[pl.ds(r0, Q_TILE), :] = _rms(y, gout_ref[...]).astype(BF16)

    first_full = PAD_ROWS // Q_TILE
    for t in range(first_full):
        tile(t, PAD_ROWS - t * Q_TILE)

    def full_body(t, carry):
        tile(t, 0)
        return carry

    lax.fori_loop(first_full, S // Q_TILE, full_body, 0, unroll=6)


def _attention(q3, k3, v3, bias_tiles, g_out):
    B, S, _ = q3.shape
    seq = pl.BlockSpec((None, S, ATT_WIDTH), lambda b: (b, 0, 0))
    return pl.pallas_call(
        _attn_kernel,
        grid=(B,),
        in_specs=[seq, seq, seq,
                  pl.BlockSpec((HEAD_GROUPS, GROUP_HEADS * Q_TILE, WINDOW), lambda b: (0, 0, 0)),
                  pl.BlockSpec((1, ATT_WIDTH), lambda b: (0, 0))],
        out_specs=seq,
        out_shape=jax.ShapeDtypeStruct((B, S, ATT_WIDTH), BF16),
        compiler_params=_params(("arbitrary",)),
        name="attn",
    )(q3, k3, v3, bias_tiles, g_out)


def _route(lt):
    n_tok = lt.shape[1]
    row = lambda j: lt[j:j + 1, :]

    gl = [row(j) for j in range(N_GROUPS)]
    gmax = jnp.maximum(jnp.maximum(gl[0], gl[1]), jnp.maximum(gl[2], gl[3]))
    grp = jnp.where(gl[0] >= gmax, 0, jnp.where(gl[1] >= gmax, 1, jnp.where(gl[2] >= gmax, 2, 3)))
    denom = (jnp.exp(gl[0] - gmax) + jnp.exp(gl[1] - gmax)
             + jnp.exp(gl[2] - gmax) + jnp.exp(gl[3] - gmax))
    p_g = 1.0 / denom

    sel = []
    for j in range(EXPERTS_PER_GROUP):
        cand = [row(N_GROUPS + g * EXPERTS_PER_GROUP + j) for g in range(N_GROUPS)]
        sel.append(jnp.where(grp == 0, cand[0],
                             jnp.where(grp == 1, cand[1], jnp.where(grp == 2, cand[2], cand[3]))))

    def first_argmax(vals):
        top = jnp.maximum(jnp.maximum(vals[0], vals[1]), jnp.maximum(vals[2], vals[3]))
        idx = jnp.where(vals[0] >= top, 0,
                        jnp.where(vals[1] >= top, 1, jnp.where(vals[2] >= top, 2, 3)))
        return top, idx

    v1, i1 = first_argmax(sel)
    rest = [jnp.where(i1 == j, -jnp.inf, sel[j]) for j in range(EXPERTS_PER_GROUP)]
    v2, i2 = first_argmax(rest)
    t = jnp.exp(v2 - v1)
    w1 = p_g * (1.0 / (1.0 + t))
    w2 = p_g * (t / (1.0 + t))
    sub = lax.broadcasted_iota(jnp.int32, (ROUTE_ROWS, n_tok), 0)
    table = jnp.zeros((ROUTE_ROWS, n_tok), F32)
    for j in range(EXPERTS_PER_GROUP):
        w_j = jnp.where(i1 == j, w1, 0.0) + jnp.where(i2 == j, w2, 0.0)
        table = jnp.where(sub == j, jnp.broadcast_to(w_j, (ROUTE_ROWS, n_tok)), table)
    return table, grp


def _outproj_kernel(ma_ref, mb_ref, w_ref, x_ref, g2_ref, wr_ref, br_ref,
                    h_ref, xn_ref, route_ref):
    mix = jnp.concatenate([ma_ref[...], mb_ref[...]], axis=1)
    h = x_ref[...] + jnp.dot(mix, w_ref[...], preferred_element_type=F32)
    h_ref[...] = h
    xn = _rms(h, g2_ref[...])
    xn_ref[:, :D_MODEL] = xn

    x_hi = xn.astype(BF16)
    x_lo = (xn - x_hi.astype(F32)).astype(BF16)
    wr = wr_ref[...]
    both = jnp.dot(x_hi, wr, preferred_element_type=F32)
    logits = (both[:, :ROUTE_LANES] + both[:, ROUTE_LANES:]
              + jnp.dot(x_lo, wr[:, :ROUTE_LANES], preferred_element_type=F32)
              + br_ref[...])
    table, grp = _route(logits.T)
    n_tok = route_ref.shape[1]
    sub = lax.broadcasted_iota(jnp.int32, (ROUTE_ROWS, n_tok), 0)
    route_ref[...] = jnp.where(sub == EXPERTS_PER_GROUP,
                               jnp.broadcast_to(grp.astype(F32), (ROUTE_ROWS, n_tok)), table)
    padded = jnp.concatenate([table, jnp.zeros((ROUTE_LANES - ROUTE_ROWS, n_tok), F32)], axis=0)
    xn_ref[:, D_MODEL:] = padded.T


def _outproj(mix_a, mix_b, w_bf, x2, g2, wr_split, br):
    T = x2.shape[0]
    row = lambda i: (i, 0)
    const = lambda i: (0, 0)
    full = pl.BlockSpec((OUT_TILE, D_MODEL), row)
    half = pl.BlockSpec((OUT_TILE, REC_WIDTH), row)
    return pl.pallas_call(
        _outproj_kernel,
        grid=(T // OUT_TILE,),
        in_specs=[half, half,
                  pl.BlockSpec((D_MODEL, D_MODEL), const),
                  full,
                  pl.BlockSpec((1, D_MODEL), const),
                  pl.BlockSpec((D_MODEL, 2 * ROUTE_LANES), const),
                  pl.BlockSpec((1, ROUTE_LANES), const)],
        out_specs=[full, pl.BlockSpec((OUT_TILE, TOKEN_ROW), row),
                   pl.BlockSpec((ROUTE_ROWS, OUT_TILE), lambda i: (0, i))],
        out_shape=[jax.ShapeDtypeStruct((T, D_MODEL), F32),
                   jax.ShapeDtypeStruct((T, TOKEN_ROW), F32),
                   jax.ShapeDtypeStruct((ROUTE_ROWS, T), F32)],
        compiler_params=_params(("arbitrary",)),
        name="outproj",
    )(mix_a, mix_b, w_bf, x2, g2, wr_split, br)


def _for_rows(n, fn):
    def body(r, carry):
        fn(r)
        return carry
    lax.fori_loop(0, n, body, 0, unroll=8)


def _experts_kernel(bgroup_ref, nvalid_ref, x_ref, wg_ref, wu_ref, wd_ref, y_ref):
    del bgroup_ref
    active = pl.program_id(0) < nvalid_ref[0]

    @pl.when(active)
    def _():
        rows = x_ref[...]
        x = rows[:, :D_MODEL].astype(BF16)
        ws = rows[:, D_MODEL:]
        hidden = []
        for j in range(EXPERTS_PER_GROUP):
            gate = jnp.dot(x, wg_ref[j], preferred_element_type=F32)
            up = jnp.dot(x, wu_ref[j], preferred_element_type=F32)
            hid = (jax.nn.silu(gate) * up) * ws[:, j:j + 1]
            hidden.append(hid.astype(BF16))
        hcat = jnp.concatenate(hidden, axis=1)
        wd = wd_ref[...].reshape(EXPERTS_PER_GROUP * D_FF_EXPERT, D_MODEL)
        y_ref[...] = jnp.dot(hcat, wd, preferred_element_type=F32)

    @pl.when(jnp.logical_not(active))
    def _():
        y_ref[...] = jnp.zeros(y_ref.shape, F32)


def _experts(bgroup, nvalid, rows_sorted, wg_bf, wu_bf, wd_bf):
    n_pad = rows_sorted.shape[0]
    grp = lambda i, bg, nv: (bg[i], 0, 0)
    row = lambda i, bg, nv: (i, 0)
    grid_spec = pltpu.PrefetchScalarGridSpec(
        num_scalar_prefetch=2,
        grid=(n_pad // EXPERT_BLOCK,),
        in_specs=[pl.BlockSpec((EXPERT_BLOCK, TOKEN_ROW), row),
                  pl.BlockSpec((EXPERTS_PER_GROUP, D_MODEL, D_FF_EXPERT), grp),
                  pl.BlockSpec((EXPERTS_PER_GROUP, D_MODEL, D_FF_EXPERT), grp),
                  pl.BlockSpec((EXPERTS_PER_GROUP, D_FF_EXPERT, D_MODEL), grp)],
        out_specs=pl.BlockSpec((EXPERT_BLOCK, D_MODEL), row),
    )
    return pl.pallas_call(
        _experts_kernel,
        grid_spec=grid_spec,
        out_shape=jax.ShapeDtypeStruct((n_pad, D_MODEL), F32),
        compiler_params=_params(("arbitrary",)),
        name="experts",
    )(bgroup, nvalid, rows_sorted, wg_bf, wu_bf, wd_bf)


def _dispatch_kernel(dest_ref, pads_ref, x_hbm, out_hbm, buf_ref, zero_ref, lsem, ssem, zsem):
    n = MOVE_TILE
    i = pl.program_id(0)
    steps = pl.num_programs(0)

    def load(t, s):
        return pltpu.make_async_copy(x_hbm.at[pl.ds(t * n, n)], buf_ref.at[s], lsem.at[s])

    def scatter(slot_row, r, s):
        return pltpu.make_async_copy(buf_ref.at[s, pl.ds(r, 1)], out_hbm.at[pl.ds(slot_row, 1)],
                                     ssem.at[s])

    def wait_scatters(s):
        pltpu.make_async_copy(buf_ref.at[s], out_hbm.at[pl.ds(0, n)], ssem.at[s]).wait()

    @pl.when(i == 0)
    def _():
        load(0, 0).start()
        load(1, 1).start()

    @pl.when(i >= 2)
    def _():
        wait_scatters((i + 2) % MOVE_RING)

    @pl.when(i + 2 < steps)
    def _():
        load(i + 2, (i + 2) % MOVE_RING).start()

    load(0, i % MOVE_RING).wait()

    def issue(cur):
        for r in range(n):
            scatter(dest_ref[i * n + r], r, cur).start(priority=r % 2)

    lax.switch(i % MOVE_RING, [functools.partial(issue, cur) for cur in range(MOVE_RING)])

    @pl.when(i == steps - 1)
    def _():
        wait_scatters((i + 3) % MOVE_RING)
        wait_scatters(i % MOVE_RING)
        zero_ref[...] = jnp.zeros(zero_ref.shape, F32)

        def fill(row):
            return pltpu.make_async_copy(zero_ref.at[pl.ds(0, 1)], out_hbm.at[pl.ds(row, 1)], zsem)

        def start_fill(row, carry):
            fill(row).start()
            return carry

        def wait_fill(row, carry):
            fill(row).wait()
            return carry

        for phase in (start_fill, wait_fill):
            for k in range(N_GROUPS + 1):
                lax.fori_loop(pads_ref[2 * k], pads_ref[2 * k + 1], phase, 0)


def _dispatch(dest, pads, rows, n_pad):
    T = rows.shape[0]
    assert T // MOVE_TILE >= MOVE_RING
    grid_spec = pltpu.PrefetchScalarGridSpec(
        num_scalar_prefetch=2,
        grid=(T // MOVE_TILE,),
        in_specs=[pl.BlockSpec(memory_space=pl.ANY)],
        out_specs=pl.BlockSpec(memory_space=pl.ANY),
        scratch_shapes=[pltpu.VMEM((MOVE_RING, MOVE_TILE, TOKEN_ROW), F32),
                        pltpu.VMEM((SUBLANES, TOKEN_ROW), F32),
                        pltpu.SemaphoreType.DMA((MOVE_RING,)),
                        pltpu.SemaphoreType.DMA((MOVE_RING,)),
                        pltpu.SemaphoreType.DMA(())],
    )
    return pl.pallas_call(
        _dispatch_kernel,
        grid_spec=grid_spec,
        out_shape=jax.ShapeDtypeStruct((n_pad, TOKEN_ROW), F32),
        compiler_params=_params(("arbitrary",)),
        name="dispatch",
    )(dest, pads, rows)


def _final_kernel(dest_ref, y_hbm, h_ref, g_ref, o_ref, ybuf_ref, gsem):
    n = MOVE_TILE
    i = pl.program_id(0)
    last = pl.num_programs(0) - 1

    def gather(slot_row, r, s):
        return pltpu.make_async_copy(y_hbm.at[pl.ds(slot_row, 1)], ybuf_ref.at[s, pl.ds(r, 1)],
                                     gsem.at[s])

    @pl.when(i == 0)
    def _():
        _for_rows(n, lambda r: gather(dest_ref[r], r, 0).start())
        _for_rows(n, lambda r: gather(dest_ref[n + r], r, 1).start())

    nxt = jnp.minimum(i + 2, last) * n

    def issue(cur):
        for r in range(n):
            gather(dest_ref[nxt + r], r, (cur + 2) % RING).start(priority=r % 2)

    lax.switch(i % RING, [functools.partial(issue, cur) for cur in range(RING)])

    def wait_gathers(s):
        pltpu.make_async_copy(y_hbm.at[pl.ds(0, n)], ybuf_ref.at[s], gsem.at[s]).wait()

    wait_gathers(i % RING)
    o_ref[...] = _rms(h_ref[...] + ybuf_ref[i % RING], g_ref[...])

    @pl.when(i == last)
    def _():
        wait_gathers((i + 1) % RING)
        wait_gathers((i + 2) % RING)


def _final(dest, y_sorted, h1, g):
    T = h1.shape[0]
    assert T // MOVE_TILE >= RING
    row = lambda i, dest: (i, 0)
    grid_spec = pltpu.PrefetchScalarGridSpec(
        num_scalar_prefetch=1,
        grid=(T // MOVE_TILE,),
        in_specs=[pl.BlockSpec(memory_space=pl.ANY),
                  pl.BlockSpec((MOVE_TILE, D_MODEL), row),
                  pl.BlockSpec((1, D_MODEL), lambda i, dest: (0, 0))],
        out_specs=pl.BlockSpec((MOVE_TILE, D_MODEL), row),
        scratch_shapes=[pltpu.VMEM((RING, MOVE_TILE, D_MODEL), F32),
                        pltpu.SemaphoreType.DMA((RING,))],
    )
    return pl.pallas_call(
        _final_kernel,
        grid_spec=grid_spec,
        out_shape=jax.ShapeDtypeStruct((T, D_MODEL), F32),
        compiler_params=_params(("arbitrary",)),
        name="final",
    )(dest, y_sorted, h1, g)


def _block_diag(w):
    h, d, _ = w.shape
    eye = jnp.eye(h, dtype=w.dtype)
    return (eye[:, None, :, None] * w[:, :, None, :]).reshape(h * d, h * d)


def _bias_table(rel_bias):
    rb = rel_bias.astype(F32)
    n_far = PAD_ROWS - MAX_REL + CHUNK
    far = jnp.broadcast_to(rb[:, 2 * MAX_REL:], (ATT_HEADS, n_far))
    near = rb[:, MAX_REL - CHUNK + 1:2 * MAX_REL][:, ::-1]
    diag = jnp.concatenate([far, near], axis=1)
    bias = jnp.stack([diag[:, CHUNK - 1 - q:CHUNK - 1 - q + BAND] for q in range(CHUNK)], axis=1)
    off = jnp.full((ATT_HEADS, CHUNK, CHUNK), -1e30, F32)
    first = jnp.concatenate([bias, off], axis=2)
    second = jnp.concatenate([off, bias], axis=2)
    tiles = jnp.concatenate([first, second], axis=1)
    return tiles.reshape(HEAD_GROUPS, GROUP_HEADS * Q_TILE, WINDOW)


def _dispatch_plan(grp, T):
    onehot = (grp[:, None] == jnp.arange(N_GROUPS)[None, :]).astype(jnp.int32)
    incl = jnp.cumsum(onehot, axis=0)
    counts = incl[-1]
    padded = (counts + EXPERT_BLOCK - 1) // EXPERT_BLOCK * EXPERT_BLOCK
    pend = jnp.cumsum(padded)
    pstart = pend - padded
    dest = jnp.sum((incl - onehot + pstart[None, :]) * onehot, axis=1).astype(jnp.int32)
    n_pad = T + N_GROUPS * EXPERT_BLOCK
    n_blocks = n_pad // EXPERT_BLOCK
    used_end = pstart + counts
    pads = jnp.stack([jnp.concatenate([used_end, pend[-1:]]),
                      jnp.concatenate([pend, jnp.full((1,), n_pad, pend.dtype)])], axis=1)
    pads = pads.reshape(-1).astype(jnp.int32)
    bstart = jnp.arange(n_blocks, dtype=jnp.int32) * EXPERT_BLOCK
    bgroup = jnp.sum((pend[None, :] <= bstart[:, None]).astype(jnp.int32), axis=1)
    bgroup = jnp.minimum(bgroup, N_GROUPS - 1)
    nvalid = (pend[-1:] // EXPERT_BLOCK).astype(jnp.int32)
    return dest, pads, bgroup, nvalid, n_pad


def kernel(x, norm1_g, w_in, conv_w, conv_b, w_rg_a, b_rg_a, w_rg_x, b_rg_x, lru_lambda,
           rel_bias, g_rec_out, g_att_out, w_out, norm2_g, w_group, b_group, w_router,
           b_router, w_e_gate, w_e_up, w_e_down, final_g):
    B, S, D = x.shape
    T = B * S
    assert w_in.shape[0] == 1, "single-layer block"
    (norm1_g, w_in, conv_w, conv_b, w_rg_a, b_rg_a, w_rg_x, b_rg_x, lru_lambda, rel_bias, g_rec_out,
     g_att_out, w_out, norm2_g, w_group, b_group, w_router, b_router, w_e_gate, w_e_up, w_e_down) = (
        a.reshape(a.shape[1:]) for a in (
            norm1_g, w_in, conv_w, conv_b, w_rg_a, b_rg_a, w_rg_x, b_rg_x, lru_lambda, rel_bias,
            g_rec_out, g_att_out, w_out, norm2_g, w_group, b_group, w_router, b_router,
            w_e_gate, w_e_up, w_e_down))
    h = x.reshape(T, D)
    n_exp, _, d_ff = w_e_gate.shape
    u, gate, q, k, v, wd_bf, wo_bf = _inproj(
        h, norm1_g.reshape(1, D), w_in, [w_e_down.reshape(n_exp * d_ff, D), w_out])

    wab = jnp.concatenate([_block_diag(w_rg_a), _block_diag(w_rg_x)], axis=1).astype(BF16)
    bab = jnp.concatenate([b_rg_a.reshape(1, REC_WIDTH), b_rg_x.reshape(1, REC_WIDTH)], axis=1)
    mix_a, wg_bf, wu_bf = _rglru(
        u.reshape(B, S, REC_WIDTH), gate.reshape(B, S, REC_WIDTH),
        conv_w, conv_b.reshape(1, REC_WIDTH), wab, bab,
        lru_lambda.reshape(1, REC_WIDTH), g_rec_out.reshape(1, REC_WIDTH),
        [w_e_gate.reshape(n_exp * D, d_ff), w_e_up.reshape(n_exp * D, d_ff)])

    mix_b = _attention(q.reshape(B, S, ATT_WIDTH), k.reshape(B, S, ATT_WIDTH),
                       v.reshape(B, S, ATT_WIDTH), _bias_table(rel_bias),
                       g_att_out.reshape(1, ATT_WIDTH))

    n_route = N_GROUPS + N_GROUPS * EXPERTS_PER_GROUP
    wr = jnp.concatenate([w_group.astype(F32), w_router.astype(F32)], axis=1)
    wr = jnp.pad(wr, ((0, 0), (0, ROUTE_LANES - n_route)))
    wr_hi = wr.astype(BF16)
    wr_lo = (wr - wr_hi.astype(F32)).astype(BF16)
    br = jnp.concatenate([b_group.astype(F32), b_router.astype(F32)])
    br = jnp.pad(br, (0, ROUTE_LANES - n_route)).reshape(1, ROUTE_LANES)
    h1, xn, route = _outproj(mix_a.reshape(T, REC_WIDTH), mix_b.reshape(T, ATT_WIDTH),
                             wo_bf, h, norm2_g.reshape(1, D),
                             jnp.concatenate([wr_hi, wr_lo], axis=1), br)

    grp = route[EXPERTS_PER_GROUP].astype(jnp.int32)
    dest, pads, bgroup, nvalid, n_pad = _dispatch_plan(grp, T)
    rows_sorted = _dispatch(dest, pads, xn, n_pad)
    y_sorted = _experts(bgroup, nvalid, rows_sorted, wg_bf.reshape(n_exp, D, d_ff),
                        wu_bf.reshape(n_exp, D, d_ff), wd_bf.reshape(n_exp, d_ff, D))
    out = _final(dest, y_sorted, h1, final_g.reshape(1, D))
    return out.reshape(B, S, D)
```

```python
import functools

import jax
import jax.numpy as jnp
from jax import lax
from jax.experimental import pallas as pl
from jax.experimental.pallas import tpu as pltpu

F32 = jnp.float32
BF16 = jnp.bfloat16

D_MODEL = 1024
CHUNK = 64
EPS = 1e-6
REC_WIDTH = 512
CONV_WIDTH = 4
LRU_C = 8.0
ATT_HEAD_DIM = 64
ATT_HEADS = 8
ATT_WIDTH = 512
LEFT_CHUNKS = 8
BAND = (LEFT_CHUNKS + 1) * CHUNK
MAX_REL = 128
IN_WIDTH = 2 * REC_WIDTH + 3 * ATT_WIDTH
N_GROUPS = 4
EXPERTS_PER_GROUP = 4
D_FF_EXPERT = 512

SUBLANES = 8
LANES = 128
VMEM_LIMIT_BYTES = 56 * 1024 * 1024

ROW_TILE = 512
OUT_TILE = 1024
SCAN_TILE = 1024
EXPERT_BLOCK = 256
MOVE_TILE = 512
RING = 3
MOVE_RING = 4
Q_TILE = 2 * CHUNK
WINDOW = BAND + CHUNK
GROUP_HEADS = 4
GROUP_LANES = GROUP_HEADS * ATT_HEAD_DIM
HEAD_GROUPS = ATT_HEADS // GROUP_HEADS
PAD_ROWS = LEFT_CHUNKS * CHUNK
ROUTE_LANES = LANES
ROUTE_ROWS = SUBLANES
TOKEN_ROW = D_MODEL + ROUTE_LANES


def _params(semantics):
    return pltpu.CompilerParams(dimension_semantics=semantics, vmem_limit_bytes=VMEM_LIMIT_BYTES)


def _rms(x, g):
    ms = jnp.mean(x * x, axis=-1, keepdims=True)
    return (x * lax.rsqrt(ms + EPS)) * g


def _for_rows(n, fn):
    def body(r, carry):
        fn(r)
        return carry
    lax.fori_loop(0, n, body, 0, unroll=8)


def _inproj_kernel(x_ref, g_ref, w_ref, *rest):
    n = (len(rest) - 6) // 2
    riders_in, outs, riders_out, wbf_ref = rest[:n], rest[n:n + 5], rest[n + 5:2 * n + 5], rest[-1]
    u_ref, gate_ref, q_ref, k_ref, v_ref = outs

    @pl.when(pl.program_id(0) == 0)
    def _():
        wbf_ref[...] = w_ref[...].astype(BF16)

    xn = _rms(x_ref[...], g_ref[...])
    z = jnp.dot(xn.astype(BF16), wbf_ref[...], preferred_element_type=F32)
    u_ref[...] = z[:, :REC_WIDTH]
    gate_ref[...] = z[:, REC_WIDTH:2 * REC_WIDTH]
    o = 2 * REC_WIDTH
    q_ref[...] = (z[:, o:o + ATT_WIDTH] * (ATT_HEAD_DIM ** -0.5)).astype(BF16)
    k_ref[...] = z[:, o + ATT_WIDTH:o + 2 * ATT_WIDTH].astype(BF16)
    v_ref[...] = z[:, o + 2 * ATT_WIDTH:o + 3 * ATT_WIDTH].astype(BF16)
    for src, dst in zip(riders_in, riders_out):
        dst[...] = src[...].astype(BF16)


def _rider_specs(arrays, n_steps, index_map):
    specs = [pl.BlockSpec((a.shape[0] // n_steps, a.shape[1]), index_map) for a in arrays]
    shapes = [jax.ShapeDtypeStruct(a.shape, BF16) for a in arrays]
    return specs, shapes


def _inproj(x2, g, w, riders):
    T = x2.shape[0]
    n_steps = T // ROW_TILE
    row = lambda i: (i, 0)
    const = lambda i: (0, 0)
    half = pl.BlockSpec((ROW_TILE, REC_WIDTH), row)
    rider_specs, rider_shapes = _rider_specs(riders, n_steps, row)
    return pl.pallas_call(
        _inproj_kernel,
        grid=(n_steps,),
        in_specs=[pl.BlockSpec((ROW_TILE, D_MODEL), row),
                  pl.BlockSpec((1, D_MODEL), const),
                  pl.BlockSpec((D_MODEL, IN_WIDTH), const)] + rider_specs,
        out_specs=[half, half, half, half, half] + rider_specs,
        out_shape=[jax.ShapeDtypeStruct((T, REC_WIDTH), F32),
                   jax.ShapeDtypeStruct((T, REC_WIDTH), F32),
                   jax.ShapeDtypeStruct((T, ATT_WIDTH), BF16),
                   jax.ShapeDtypeStruct((T, ATT_WIDTH), BF16),
                   jax.ShapeDtypeStruct((T, ATT_WIDTH), BF16)] + rider_shapes,
        scratch_shapes=[pltpu.VMEM((D_MODEL, IN_WIDTH), BF16)],
        compiler_params=_params(("arbitrary",)),
        name="inproj",
    )(x2, g, w, *riders)


def _rglru_kernel(u_ref, gate_ref, convw_ref, convb_ref, wab_ref, bab_ref, lam_ref, gout_ref, *rest):
    n_riders = (len(rest) - 5) // 2
    riders_in, o_ref = rest[:n_riders], rest[n_riders]
    riders_out = rest[n_riders + 1:2 * n_riders + 1]
    tail_ref, a_ref, b_ref, carry_ref = rest[-4:]
    for src, dst in zip(riders_in, riders_out):
        dst[...] = src[...].astype(BF16)
    n = SCAN_TILE

    @pl.when(pl.program_id(1) == 0)
    def _():
        tail_ref[...] = jnp.zeros((SUBLANES, REC_WIDTH), F32)
        carry_ref[...] = jnp.zeros((SUBLANES, REC_WIDTH), F32)

    groups = n // SUBLANES
    shape3 = (groups, SUBLANES, REC_WIDTH)
    row8 = lax.broadcasted_iota(jnp.int32, shape3, 1)
    u3 = u_ref[...].reshape(shape3)
    ext3 = jnp.concatenate([tail_ref[...].reshape(1, SUBLANES, REC_WIDTH), u3], axis=0)
    tail_ref[...] = u3[groups - 1]
    uc3 = convb_ref[...].reshape(1, 1, REC_WIDTH)
    for j in range(CONV_WIDTH):
        shift = CONV_WIDTH - 1 - j
        if shift:
            rolled = pltpu.roll(ext3, shift, 1)
            x = jnp.where(row8 >= shift, rolled[1:], rolled[:-1])
        else:
            x = u3
        uc3 = uc3 + x * convw_ref[j:j + 1, :].reshape(1, 1, REC_WIDTH)
    uc = uc3.reshape(n, REC_WIDTH)

    ri = jnp.dot(uc.astype(BF16), wab_ref[...], preferred_element_type=F32) + bab_ref[...]
    r = jax.nn.sigmoid(ri[:, :REC_WIDTH])
    ig = jax.nn.sigmoid(ri[:, REC_WIDTH:])
    log_a = (-LRU_C * r) * jax.nn.softplus(-lam_ref[...])
    a = jnp.exp(log_a)
    bx = jnp.sqrt(-jnp.tanh(log_a) * (a * a + 1.0)) * (ig * uc)

    a = a.reshape(shape3)
    bx = bx.reshape(shape3)
    for s in (1, 2, 4):
        keep = row8 >= s
        a_prev = jnp.where(keep, pltpu.roll(a, s, 1), 1.0)
        b_prev = jnp.where(keep, pltpu.roll(bx, s, 1), 0.0)
        bx = a * b_prev + bx
        a = a * a_prev
    a_ref[...] = a.reshape(n, REC_WIDTH)
    b_ref[...] = bx.reshape(n, REC_WIDTH)

    def group(gi, carry):
        r0 = pl.multiple_of(gi * SUBLANES, SUBLANES)
        h = b_ref[pl.ds(r0, SUBLANES), :] + a_ref[pl.ds(r0, SUBLANES), :] * carry
        b_ref[pl.ds(r0, SUBLANES), :] = h
        return jnp.broadcast_to(h[SUBLANES - 1:SUBLANES, :], (SUBLANES, REC_WIDTH))

    carry_ref[...] = lax.fori_loop(0, n // SUBLANES, group, carry_ref[...], unroll=4)

    y = b_ref[...] * jax.nn.gelu(gate_ref[...])
    o_ref[...] = _rms(y, gout_ref[...]).astype(BF16)


def _rglru(u3, gate3, conv_w, conv_b, wab_bf, bab, lam, g_out, riders):
    B, S, _ = u3.shape
    n_tiles = S // SCAN_TILE
    tile = pl.BlockSpec((None, SCAN_TILE, REC_WIDTH), lambda b, i: (b, i, 0))
    const = lambda b, i: (0, 0)
    vec = pl.BlockSpec((1, REC_WIDTH), const)
    rider_specs, rider_shapes = _rider_specs(riders, B * n_tiles, lambda b, i: (b * n_tiles + i, 0))
    return pl.pallas_call(
        _rglru_kernel,
        grid=(B, n_tiles),
        in_specs=[tile, tile,
                  pl.BlockSpec((CONV_WIDTH, REC_WIDTH), const), vec,
                  pl.BlockSpec((REC_WIDTH, 2 * REC_WIDTH), const),
                  pl.BlockSpec((1, 2 * REC_WIDTH), const), vec, vec] + rider_specs,
        out_specs=[tile] + rider_specs,
        out_shape=[jax.ShapeDtypeStruct((B, S, REC_WIDTH), BF16)] + rider_shapes,
        scratch_shapes=[pltpu.VMEM((SUBLANES, REC_WIDTH), F32),
                        pltpu.VMEM((SCAN_TILE, REC_WIDTH), F32),
                        pltpu.VMEM((SCAN_TILE, REC_WIDTH), F32),
                        pltpu.VMEM((SUBLANES, REC_WIDTH), F32)],
        compiler_params=_params(("arbitrary", "arbitrary")),
        name="rglru",
    )(u3, gate3, conv_w, conv_b, wab_bf, bab, lam, g_out, *riders)


def _attn_kernel(q_ref, k_ref, v_ref, bias_ref, gout_ref, o_ref):
    S = q_ref.shape[0]
    lane_head = lax.broadcasted_iota(jnp.int32, (Q_TILE, GROUP_LANES), 1) // ATT_HEAD_DIM

    def tile(t, skip):
        r0 = t * Q_TILE
        k0 = r0 - PAD_ROWS + skip
        if not isinstance(t, int):
            r0 = pl.multiple_of(r0, Q_TILE)
            k0 = pl.multiple_of(k0, Q_TILE)
        q_rows = q_ref[pl.ds(r0, Q_TILE), :]
        k_win = k_ref[pl.ds(k0, WINDOW - skip), :]
        v_win = v_ref[pl.ds(k0, WINDOW - skip), :]
        outs = []
        for hg in range(HEAD_GROUPS):
            sl = slice(hg * GROUP_LANES, (hg + 1) * GROUP_LANES)
            q4 = q_rows[:, sl]
            zero = jnp.zeros_like(q4)
            q_stack = jnp.concatenate(
                [jnp.where(lane_head == j, q4, zero) for j in range(GROUP_HEADS)], axis=0)
            s = lax.dot_general(q_stack, k_win[:, sl], (((1,), (1,)), ((), ())),
                                preferred_element_type=F32)
            s = s + bias_ref[hg][:, skip:]
            m = jnp.max(s, axis=-1, keepdims=True)
            e = jnp.exp(s - m)
            denom = jnp.sum(e, axis=-1, keepdims=True)
            o = jnp.dot(e.astype(BF16), v_win[:, sl], preferred_element_type=F32)
            o = o / denom
            out = o[:Q_TILE]
            for j in range(1, GROUP_HEADS):
                out = jnp.where(lane_head == j, o[j * Q_TILE:(j + 1) * Q_TILE], out)
            outs.append(out)
        y = jnp.concatenate(outs, axis=1)
        o_ref[pl.ds(r0, Q_TILE), :] = _rms(y, gout_ref[...]).astype(BF16)

    first_full = PAD_ROWS // Q_TILE
    for t in range(first_full):
        tile(t, PAD_ROWS - t * Q_TILE)

    def full_body(t, carry):
        tile(t, 0)
        return carry

    lax.fori_loop(first_full, S // Q_TILE, full_body, 0, unroll=6)


def _attention(q3, k3, v3, bias_tiles, g_out):
    B, S, _ = q3.shape
    seq = pl.BlockSpec((None, S, ATT_WIDTH), lambda b: (b, 0, 0))
    return pl.pallas_call(
        _attn_kernel,
        grid=(B,),
        in_specs=[seq, seq, seq,
                  pl.BlockSpec((HEAD_GROUPS, GROUP_HEADS * Q_TILE, WINDOW), lambda b: (0, 0, 0)),
                  pl.BlockSpec((1, ATT_WIDTH), lambda b: (0, 0))],
        out_specs=seq,
        out_shape=jax.ShapeDtypeStruct((B, S, ATT_WIDTH), BF16),
        compiler_params=_params(("arbitrary",)),
        name="attn",
    )(q3, k3, v3, bias_tiles, g_out)


def _route(lt):
    n_tok = lt.shape[1]
    row = lambda j: lt[j:j + 1, :]

    gl = [row(j) for j in range(N_GROUPS)]
    gmax = jnp.maximum(jnp.maximum(gl[0], gl[1]), jnp.maximum(gl[2], gl[3]))
    grp = jnp.where(gl[0] >= gmax, 0, jnp.where(gl[1] >= gmax, 1, jnp.where(gl[2] >= gmax, 2, 3)))
    denom = (jnp.exp(gl[0] - gmax) + jnp.exp(gl[1] - gmax)
             + jnp.exp(gl[2] - gmax) + jnp.exp(gl[3] - gmax))
    p_g = 1.0 / denom

    sel = []
    for j in range(EXPERTS_PER_GROUP):
        cand = [row(N_GROUPS + g * EXPERTS_PER_GROUP + j) for g in range(N_GROUPS)]
        sel.append(jnp.where(grp == 0, cand[0],
                             jnp.where(grp == 1, cand[1], jnp.where(grp == 2, cand[2], cand[3]))))

    def first_argmax(vals):
        top = jnp.maximum(jnp.maximum(vals[0], vals[1]), jnp.maximum(vals[2], vals[3]))
        idx = jnp.where(vals[0] >= top, 0,
                        jnp.where(vals[1] >= top, 1, jnp.where(vals[2] >= top, 2, 3)))
        return top, idx

    v1, i1 = first_argmax(sel)
    rest = [jnp.where(i1 == j, -jnp.inf, sel[j]) for j in range(EXPERTS_PER_GROUP)]
    v2, i2 = first_argmax(rest)
    t = jnp.exp(v2 - v1)
    w1 = p_g * (1.0 / (1.0 + t))
    w2 = p_g * (t / (1.0 + t))
    sub = lax.broadcasted_iota(jnp.int32, (ROUTE_ROWS, n_tok), 0)
    table = jnp.zeros((ROUTE_ROWS, n_tok), F32)
    for j in range(EXPERTS_PER_GROUP):
        w_j = jnp.where(i1 == j, w1, 0.0) + jnp.where(i2 == j, w2, 0.0)
        table = jnp.where(sub == j, jnp.broadcast_to(w_j, (ROUTE_ROWS, n_tok)), table)
    return table, grp


def _outproj_kernel(ma_ref, mb_ref, w_ref, x_ref, g2_ref, wr_ref, br_ref,
                    h_ref, xn_ref, route_ref):
    mix = jnp.concatenate([ma_ref[...], mb_ref[...]], axis=1)
    h = x_ref[...] + jnp.dot(mix, w_ref[...], preferred_element_type=F32)
    h_ref[...] = h
    xn = _rms(h, g2_ref[...])
    xn_ref[:, :D_MODEL] = xn

    x_hi = xn.astype(BF16)
    x_lo = (xn - x_hi.astype(F32)).astype(BF16)
    wr = wr_ref[...]
    both = jnp.dot(x_hi, wr, preferred_element_type=F32)
    logits = (both[:, :ROUTE_LANES] + both[:, ROUTE_LANES:]
              + jnp.dot(x_lo, wr[:, :ROUTE_LANES], preferred_element_type=F32)
              + br_ref[...])
    table, grp = _route(logits.T)
    n_tok = route_ref.shape[1]
    sub = lax.broadcasted_iota(jnp.int32, (ROUTE_ROWS, n_tok), 0)
    route_ref[...] = jnp.where(sub == EXPERTS_PER_GROUP,
                               jnp.broadcast_to(grp.astype(F32), (ROUTE_ROWS, n_tok)), table)
    padded = jnp.concatenate([table, jnp.zeros((ROUTE_LANES - ROUTE_ROWS, n_tok), F32)], axis=0)
    xn_ref[:, D_MODEL:] = padded.T


def _outproj(mix_a, mix_b, w_bf, x2, g2, wr_split, br):
    T = x2.shape[0]
    row = lambda i: (i, 0)
    const = lambda i: (0, 0)
    full = pl.BlockSpec((OUT_TILE, D_MODEL), row)
    half = pl.BlockSpec((OUT_TILE, REC_WIDTH), row)
    return pl.pallas_call(
        _outproj_kernel,
        grid=(T // OUT_TILE,),
        in_specs=[half, half,
                  pl.BlockSpec((D_MODEL, D_MODEL), const),
                  full,
                  pl.BlockSpec((1, D_MODEL), const),
                  pl.BlockSpec((D_MODEL, 2 * ROUTE_LANES), const),
                  pl.BlockSpec((1, ROUTE_LANES), const)],
        out_specs=[full, pl.BlockSpec((OUT_TILE, TOKEN_ROW), row),
                   pl.BlockSpec((ROUTE_ROWS, OUT_TILE), lambda i: (0, i))],
        out_shape=[jax.ShapeDtypeStruct((T, D_MODEL), F32),
                   jax.ShapeDtypeStruct((T, TOKEN_ROW), F32),
                   jax.ShapeDtypeStruct((ROUTE_ROWS, T), F32)],
        compiler_params=_params(("arbitrary",)),
        name="outproj",
    )(mix_a, mix_b, w_bf, x2, g2, wr_split, br)


def _experts_kernel(bgroup_ref, nvalid_ref, x_ref, wg_ref, wu_ref, wd_ref, y_ref):
    del bgroup_ref
    active = pl.program_id(0) < nvalid_ref[0]

    @pl.when(active)
    def _():
        rows = x_ref[...]
        x = rows[:, :D_MODEL].astype(BF16)
        ws = rows[:, D_MODEL:]
        hidden = []
        for j in range(EXPERTS_PER_GROUP):
            gate = jnp.dot(x, wg_ref[j], preferred_element_type=F32)
            up = jnp.dot(x, wu_ref[j], preferred_element_type=F32)
            hid = (jax.nn.silu(gate) * up) * ws[:, j:j + 1]
            hidden.append(hid.astype(BF16))
        hcat = jnp.concatenate(hidden, axis=1)
        wd = wd_ref[...].reshape(EXPERTS_PER_GROUP * D_FF_EXPERT, D_MODEL)
        y_ref[...] = jnp.dot(hcat, wd, preferred_element_type=F32)

    @pl.when(jnp.logical_not(active))
    def _():
        y_ref[...] = jnp.zeros(y_ref.shape, F32)


def _experts(bgroup, nvalid, rows_sorted, wg_bf, wu_bf, wd_bf):
    n_pad = rows_sorted.shape[0]
    grp = lambda i, bg, nv: (bg[i], 0, 0)
    row = lambda i, bg, nv: (i, 0)
    grid_spec = pltpu.PrefetchScalarGridSpec(
        num_scalar_prefetch=2,
        grid=(n_pad // EXPERT_BLOCK,),
        in_specs=[pl.BlockSpec((EXPERT_BLOCK, TOKEN_ROW), row),
                  pl.BlockSpec((EXPERTS_PER_GROUP, D_MODEL, D_FF_EXPERT), grp),
                  pl.BlockSpec((EXPERTS_PER_GROUP, D_MODEL, D_FF_EXPERT), grp),
                  pl.BlockSpec((EXPERTS_PER_GROUP, D_FF_EXPERT, D_MODEL), grp)],
        out_specs=pl.BlockSpec((EXPERT_BLOCK, D_MODEL), row),
    )
    return pl.pallas_call(
        _experts_kernel,
        grid_spec=grid_spec,
        out_shape=jax.ShapeDtypeStruct((n_pad, D_MODEL), F32),
        compiler_params=_params(("arbitrary",)),
        name="experts",
    )(bgroup, nvalid, rows_sorted, wg_bf, wu_bf, wd_bf)


def _dispatch_kernel(dest_ref, pads_ref, x_hbm, out_hbm, buf_ref, zero_ref, lsem, ssem, zsem):
    n = MOVE_TILE
    i = pl.program_id(0)
    steps = pl.num_programs(0)

    def load(t, s):
        return pltpu.make_async_copy(x_hbm.at[pl.ds(t * n, n)], buf_ref.at[s], lsem.at[s])

    def scatter(slot_row, r, s):
        return pltpu.make_async_copy(buf_ref.at[s, pl.ds(r, 1)], out_hbm.at[pl.ds(slot_row, 1)],
                                     ssem.at[s])

    def wait_scatters(s):
        pltpu.make_async_copy(buf_ref.at[s], out_hbm.at[pl.ds(0, n)], ssem.at[s]).wait()

    @pl.when(i == 0)
    def _():
        load(0, 0).start()
        load(1, 1).start()

    @pl.when(i >= 2)
    def _():
        wait_scatters((i + 2) % MOVE_RING)

    @pl.when(i + 2 < steps)
    def _():
        load(i + 2, (i + 2) % MOVE_RING).start()

    load(0, i % MOVE_RING).wait()

    def issue(cur):
        for r in range(n):
            scatter(dest_ref[i * n + r], r, cur).start(priority=r % 2)

    lax.switch(i % MOVE_RING, [functools.partial(issue, cur) for cur in range(MOVE_RING)])

    @pl.when(i == steps - 1)
    def _():
        wait_scatters((i + 3) % MOVE_RING)
        wait_scatters(i % MOVE_RING)
        zero_ref[...] = jnp.zeros(zero_ref.shape, F32)

        def fill(row):
            return pltpu.make_async_copy(zero_ref.at[pl.ds(0, 1)], out_hbm.at[pl.ds(row, 1)], zsem)

        def start_fill(row, carry):
            fill(row).start()
            return carry

        def wait_fill(row, carry):
            fill(row).wait()
            return carry

        for phase in (start_fill, wait_fill):
            for k in range(N_GROUPS + 1):
                lax.fori_loop(pads_ref[2 * k], pads_ref[2 * k + 1], phase, 0)


def _dispatch(dest, pads, rows, n_pad):
    T = rows.shape[0]
    assert T // MOVE_TILE >= MOVE_RING
    grid_spec = pltpu.PrefetchScalarGridSpec(
        num_scalar_prefetch=2,
        grid=(T // MOVE_TILE,),
        in_specs=[pl.BlockSpec(memory_space=pl.ANY)],
        out_specs=pl.BlockSpec(memory_space=pl.ANY),
        scratch_shapes=[pltpu.VMEM((MOVE_RING, MOVE_TILE, TOKEN_ROW), F32),
                        pltpu.VMEM((SUBLANES, TOKEN_ROW), F32),
                        pltpu.SemaphoreType.DMA((MOVE_RING,)),
                        pltpu.SemaphoreType.DMA((MOVE_RING,)),
                        pltpu.SemaphoreType.DMA(())],
    )
    return pl.pallas_call(
        _dispatch_kernel,
        grid_spec=grid_spec,
        out_shape=jax.ShapeDtypeStruct((n_pad, TOKEN_ROW), F32),
        compiler_params=_params(("arbitrary",)),
        name="dispatch",
    )(dest, pads, rows)


def _final_kernel(dest_ref, y_hbm, h_ref, g_ref, o_ref, ybuf_ref, gsem):
    n = MOVE_TILE
    i = pl.program_id(0)
    last = pl.num_programs(0) - 1

    def gather(slot_row, r, s):
        return pltpu.make_async_copy(y_hbm.at[pl.ds(slot_row, 1)], ybuf_ref.at[s, pl.ds(r, 1)],
                                     gsem.at[s])

    @pl.when(i == 0)
    def _():
        _for_rows(n, lambda r: gather(dest_ref[r], r, 0).start())
        _for_rows(n, lambda r: gather(dest_ref[n + r], r, 1).start())

    nxt = jnp.minimum(i + 2, last) * n

    def wait_gathers(s):
        pltpu.make_async_copy(y_hbm.at[pl.ds(0, n)], ybuf_ref.at[s], gsem.at[s]).wait()

    wait_gathers(i % RING)

    def step(cur):
        for r in range(n):
            gather(dest_ref[nxt + r], r, (cur + 2) % RING).start(priority=r % 2)
        o_ref[...] = _rms(h_ref[...] + ybuf_ref[cur], g_ref[...])

    lax.switch(i % RING, [functools.partial(step, cur) for cur in range(RING)])

    @pl.when(i == last)
    def _():
        wait_gathers((i + 1) % RING)
        wait_gathers((i + 2) % RING)


def _final(dest, y_sorted, h1, g):
    T = h1.shape[0]
    assert T // MOVE_TILE >= RING
    row = lambda i, dest: (i, 0)
    grid_spec = pltpu.PrefetchScalarGridSpec(
        num_scalar_prefetch=1,
        grid=(T // MOVE_TILE,),
        in_specs=[pl.BlockSpec(memory_space=pl.ANY),
                  pl.BlockSpec((MOVE_TILE, D_MODEL), row),
                  pl.BlockSpec((1, D_MODEL), lambda i, dest: (0, 0))],
        out_specs=pl.BlockSpec((MOVE_TILE, D_MODEL), row),
        scratch_shapes=[pltpu.VMEM((RING, MOVE_TILE, D_MODEL), F32),
                        pltpu.SemaphoreType.DMA((RING,))],
    )
    return pl.pallas_call(
        _final_kernel,
        grid_spec=grid_spec,
        out_shape=jax.ShapeDtypeStruct((T, D_MODEL), F32),
        compiler_params=_params(("arbitrary",)),
        name="final",
    )(dest, y_sorted, h1, g)


def _block_diag(w):
    h, d, _ = w.shape
    eye = jnp.eye(h, dtype=w.dtype)
    return (eye[:, None, :, None] * w[:, :, None, :]).reshape(h * d, h * d)


def _bias_table(rel_bias):
    rb = rel_bias.astype(F32)
    n_far = PAD_ROWS - MAX_REL + CHUNK
    far = jnp.broadcast_to(rb[:, 2 * MAX_REL:], (ATT_HEADS, n_far))
    near = rb[:, MAX_REL - CHUNK + 1:2 * MAX_REL][:, ::-1]
    diag = jnp.concatenate([far, near], axis=1)
    bias = jnp.stack([diag[:, CHUNK - 1 - q:CHUNK - 1 - q + BAND] for q in range(CHUNK)], axis=1)
    off = jnp.full((ATT_HEADS, CHUNK, CHUNK), -1e30, F32)
    first = jnp.concatenate([bias, off], axis=2)
    second = jnp.concatenate([off, bias], axis=2)
    tiles = jnp.concatenate([first, second], axis=1)
    return tiles.reshape(HEAD_GROUPS, GROUP_HEADS * Q_TILE, WINDOW)


def _dispatch_plan(grp, T):
    onehot = (grp[:, None] == jnp.arange(N_GROUPS)[None, :]).astype(jnp.int32)
    incl = jnp.cumsum(onehot, axis=0)
    counts = incl[-1]
    padded = (counts + EXPERT_BLOCK - 1) // EXPERT_BLOCK * EXPERT_BLOCK
    pend = jnp.cumsum(padded)
    pstart = pend - padded
    dest = jnp.sum((incl - onehot + pstart[None, :]) * onehot, axis=1).astype(jnp.int32)
    n_pad = T + N_GROUPS * EXPERT_BLOCK
    n_blocks = n_pad // EXPERT_BLOCK
    used_end = pstart + counts
    pads = jnp.stack([jnp.concatenate([used_end, pend[-1:]]),
                      jnp.concatenate([pend, jnp.full((1,), n_pad, pend.dtype)])], axis=1)
    pads = pads.reshape(-1).astype(jnp.int32)
    bstart = jnp.arange(n_blocks, dtype=jnp.int32) * EXPERT_BLOCK
    bgroup = jnp.sum((pend[None, :] <= bstart[:, None]).astype(jnp.int32), axis=1)
    bgroup = jnp.minimum(bgroup, N_GROUPS - 1)
    nvalid = (pend[-1:] // EXPERT_BLOCK).astype(jnp.int32)
    return dest, pads, bgroup, nvalid, n_pad


def kernel(x, norm1_g, w_in, conv_w, conv_b, w_rg_a, b_rg_a, w_rg_x, b_rg_x, lru_lambda,
           rel_bias, g_rec_out, g_att_out, w_out, norm2_g, w_group, b_group, w_router,
           b_router, w_e_gate, w_e_up, w_e_down, final_g):
    B, S, D = x.shape
    T = B * S
    assert w_in.shape[0] == 1, "single-layer block"
    (norm1_g, w_in, conv_w, conv_b, w_rg_a, b_rg_a, w_rg_x, b_rg_x, lru_lambda, rel_bias, g_rec_out,
     g_att_out, w_out, norm2_g, w_group, b_group, w_router, b_router, w_e_gate, w_e_up, w_e_down) = (
        a.reshape(a.shape[1:]) for a in (
            norm1_g, w_in, conv_w, conv_b, w_rg_a, b_rg_a, w_rg_x, b_rg_x, lru_lambda, rel_bias,
            g_rec_out, g_att_out, w_out, norm2_g, w_group, b_group, w_router, b_router,
            w_e_gate, w_e_up, w_e_down))
    h = x.reshape(T, D)
    n_exp, _, d_ff = w_e_gate.shape
    u, gate, q, k, v, wd_bf, wo_bf = _inproj(
        h, norm1_g.reshape(1, D), w_in, [w_e_down.reshape(n_exp * d_ff, D), w_out])

    wab = jnp.concatenate([_block_diag(w_rg_a), _block_diag(w_rg_x)], axis=1).astype(BF16)
    bab = jnp.concatenate([b_rg_a.reshape(1, REC_WIDTH), b_rg_x.reshape(1, REC_WIDTH)], axis=1)
    mix_a, wg_bf, wu_bf = _rglru(
        u.reshape(B, S, REC_WIDTH), gate.reshape(B, S, REC_WIDTH),
        conv_w, conv_b.reshape(1, REC_WIDTH), wab, bab,
        lru_lambda.reshape(1, REC_WIDTH), g_rec_out.reshape(1, REC_WIDTH),
        [w_e_gate.reshape(n_exp * D, d_ff), w_e_up.reshape(n_exp * D, d_ff)])

    mix_b = _attention(q.reshape(B, S, ATT_WIDTH), k.reshape(B, S, ATT_WIDTH),
                       v.reshape(B, S, ATT_WIDTH), _bias_table(rel_bias),
                       g_att_out.reshape(1, ATT_WIDTH))

    n_route = N_GROUPS + N_GROUPS * EXPERTS_PER_GROUP
    wr = jnp.concatenate([w_group.astype(F32), w_router.astype(F32)], axis=1)
    wr = jnp.pad(wr, ((0, 0), (0, ROUTE_LANES - n_route)))
    wr_hi = wr.astype(BF16)
    wr_lo = (wr - wr_hi.astype(F32)).astype(BF16)
    br = jnp.concatenate([b_group.astype(F32), b_router.astype(F32)])
    br = jnp.pad(br, (0, ROUTE_LANES - n_route)).reshape(1, ROUTE_LANES)
    h1, xn, route = _outproj(mix_a.reshape(T, REC_WIDTH), mix_b.reshape(T, ATT_WIDTH),
                             wo_bf, h, norm2_g.reshape(1, D),
                             jnp.concatenate([wr_hi, wr_lo], axis=1), br)

    grp = route[EXPERTS_PER_GROUP].astype(jnp.int32)
    dest, pads, bgroup, nvalid, n_pad = _dispatch_plan(grp, T)
    rows_sorted = _dispatch(dest, pads, xn, n_pad)
    y_sorted = _experts(bgroup, nvalid, rows_sorted, wg_bf.reshape(n_exp, D, d_ff),
                        wu_bf.reshape(n_exp, D, d_ff), wd_bf.reshape(n_exp, d_ff, D))
    out = _final(dest, y_sorted, h1, final_g.reshape(1, D))
    return out.reshape(B, S, D)
```

```python
import functools

import jax
import jax.numpy as jnp
from jax import lax
from jax.experimental import pallas as pl
from jax.experimental.pallas import tpu as pltpu

F32 = jnp.float32
BF16 = jnp.bfloat16

D_MODEL = 1024
CHUNK = 64
EPS = 1e-6
REC_WIDTH = 512
CONV_WIDTH = 4
LRU_C = 8.0
ATT_HEAD_DIM = 64
ATT_HEADS = 8
ATT_WIDTH = 512
LEFT_CHUNKS = 8
BAND = (LEFT_CHUNKS + 1) * CHUNK
MAX_REL = 128
IN_WIDTH = 2 * REC_WIDTH + 3 * ATT_WIDTH
N_GROUPS = 4
EXPERTS_PER_GROUP = 4
D_FF_EXPERT = 512

SUBLANES = 8
LANES = 128
VMEM_LIMIT_BYTES = 56 * 1024 * 1024

ROW_TILE = 512
OUT_TILE = 1024
SCAN_TILE = 1024
EXPERT_BLOCK = 512
MOVE_TILE = 512
RING = 3
MOVE_RING = 4
Q_TILE = 2 * CHUNK
WINDOW = BAND + CHUNK
GROUP_HEADS = 4
GROUP_LANES = GROUP_HEADS * ATT_HEAD_DIM
HEAD_GROUPS = ATT_HEADS // GROUP_HEADS
PAD_ROWS = LEFT_CHUNKS * CHUNK
ROUTE_LANES = LANES
ROUTE_ROWS = SUBLANES
TOKEN_ROW = D_MODEL + ROUTE_LANES


def _params(semantics):
    return pltpu.CompilerParams(dimension_semantics=semantics, vmem_limit_bytes=VMEM_LIMIT_BYTES)


def _rms(x, g):
    ms = jnp.mean(x * x, axis=-1, keepdims=True)
    return (x * lax.rsqrt(ms + EPS)) * g


def _for_rows(n, fn):
    def body(r, carry):
        fn(r)
        return carry
    lax.fori_loop(0, n, body, 0, unroll=8)


def _inproj_kernel(x_ref, g_ref, w_ref, *rest):
    n = (len(rest) - 6) // 2
    riders_in, outs, riders_out, wbf_ref = rest[:n], rest[n:n + 5], rest[n + 5:2 * n + 5], rest[-1]
    u_ref, gate_ref, q_ref, k_ref, v_ref = outs

    @pl.when(pl.program_id(0) == 0)
    def _():
        wbf_ref[...] = w_ref[...].astype(BF16)

    xn = _rms(x_ref[...], g_ref[...])
    z = jnp.dot(xn.astype(BF16), wbf_ref[...], preferred_element_type=F32)
    u_ref[...] = z[:, :REC_WIDTH]
    gate_ref[...] = z[:, REC_WIDTH:2 * REC_WIDTH]
    o = 2 * REC_WIDTH
    q_ref[...] = (z[:, o:o + ATT_WIDTH] * (ATT_HEAD_DIM ** -0.5)).astype(BF16)
    k_ref[...] = z[:, o + ATT_WIDTH:o + 2 * ATT_WIDTH].astype(BF16)
    v_ref[...] = z[:, o + 2 * ATT_WIDTH:o + 3 * ATT_WIDTH].astype(BF16)
    for src, dst in zip(riders_in, riders_out):
        dst[...] = src[...].astype(BF16)


def _rider_specs(arrays, n_steps, index_map):
    specs = [pl.BlockSpec((a.shape[0] // n_steps, a.shape[1]), index_map) for a in arrays]
    shapes = [jax.ShapeDtypeStruct(a.shape, BF16) for a in arrays]
    return specs, shapes


def _inproj(x2, g, w, riders):
    T = x2.shape[0]
    n_steps = T // ROW_TILE
    row = lambda i: (i, 0)
    const = lambda i: (0, 0)
    half = pl.BlockSpec((ROW_TILE, REC_WIDTH), row)
    rider_specs, rider_shapes = _rider_specs(riders, n_steps, row)
    return pl.pallas_call(
        _inproj_kernel,
        grid=(n_steps,),
        in_specs=[pl.BlockSpec((ROW_TILE, D_MODEL), row),
                  pl.BlockSpec((1, D_MODEL), const),
                  pl.BlockSpec((D_MODEL, IN_WIDTH), const)] + rider_specs,
        out_specs=[half, half, half, half, half] + rider_specs,
        out_shape=[jax.ShapeDtypeStruct((T, REC_WIDTH), F32),
                   jax.ShapeDtypeStruct((T, REC_WIDTH), F32),
                   jax.ShapeDtypeStruct((T, ATT_WIDTH), BF16),
                   jax.ShapeDtypeStruct((T, ATT_WIDTH), BF16),
                   jax.ShapeDtypeStruct((T, ATT_WIDTH), BF16)] + rider_shapes,
        scratch_shapes=[pltpu.VMEM((D_MODEL, IN_WIDTH), BF16)],
        compiler_params=_params(("arbitrary",)),
        name="inproj",
    )(x2, g, w, *riders)


def _rglru_kernel(u_ref, gate_ref, convw_ref, convb_ref, wab_ref, bab_ref, lam_ref, gout_ref, *rest):
    n_riders = (len(rest) - 5) // 2
    riders_in, o_ref = rest[:n_riders], rest[n_riders]
    riders_out = rest[n_riders + 1:2 * n_riders + 1]
    tail_ref, a_ref, b_ref, carry_ref = rest[-4:]
    for src, dst in zip(riders_in, riders_out):
        dst[...] = src[...].astype(BF16)
    n = SCAN_TILE

    @pl.when(pl.program_id(1) == 0)
    def _():
        tail_ref[...] = jnp.zeros((SUBLANES, REC_WIDTH), F32)
        carry_ref[...] = jnp.zeros((SUBLANES, REC_WIDTH), F32)

    groups = n // SUBLANES
    shape3 = (groups, SUBLANES, REC_WIDTH)
    row8 = lax.broadcasted_iota(jnp.int32, shape3, 1)
    u3 = u_ref[...].reshape(shape3)
    ext3 = jnp.concatenate([tail_ref[...].reshape(1, SUBLANES, REC_WIDTH), u3], axis=0)
    tail_ref[...] = u3[groups - 1]
    uc3 = convb_ref[...].reshape(1, 1, REC_WIDTH)
    for j in range(CONV_WIDTH):
        shift = CONV_WIDTH - 1 - j
        if shift:
            rolled = pltpu.roll(ext3, shift, 1)
            x = jnp.where(row8 >= shift, rolled[1:], rolled[:-1])
        else:
            x = u3
        uc3 = uc3 + x * convw_ref[j:j + 1, :].reshape(1, 1, REC_WIDTH)
    uc = uc3.reshape(n, REC_WIDTH)

    ri = jnp.dot(uc.astype(BF16), wab_ref[...], preferred_element_type=F32) + bab_ref[...]
    r = jax.nn.sigmoid(ri[:, :REC_WIDTH])
    ig = jax.nn.sigmoid(ri[:, REC_WIDTH:])
    log_a = (-LRU_C * r) * jax.nn.softplus(-lam_ref[...])
    a = jnp.exp(log_a)
    bx = jnp.sqrt(-jnp.tanh(log_a) * (a * a + 1.0)) * (ig * uc)

    a = a.reshape(shape3)
    bx = bx.reshape(shape3)
    for s in (1, 2, 4):
        keep = row8 >= s
        a_prev = jnp.where(keep, pltpu.roll(a, s, 1), 1.0)
        b_prev = jnp.where(keep, pltpu.roll(bx, s, 1), 0.0)
        bx = a * b_prev + bx
        a = a * a_prev
    a_ref[...] = a.reshape(n, REC_WIDTH)
    b_ref[...] = bx.reshape(n, REC_WIDTH)

    def group(gi, carry):
        r0 = pl.multiple_of(gi * SUBLANES, SUBLANES)
        h = b_ref[pl.ds(r0, SUBLANES), :] + a_ref[pl.ds(r0, SUBLANES), :] * carry
        b_ref[pl.ds(r0, SUBLANES), :] = h
        return jnp.broadcast_to(h[SUBLANES - 1:SUBLANES, :], (SUBLANES, REC_WIDTH))

    carry_ref[...] = lax.fori_loop(0, n // SUBLANES, group, carry_ref[...], unroll=4)

    y = b_ref[...] * jax.nn.gelu(gate_ref[...])
    o_ref[...] = _rms(y, gout_ref[...]).astype(BF16)


def _rglru(u3, gate3, conv_w, conv_b, wab_bf, bab, lam, g_out, riders):
    B, S, _ = u3.shape
    n_tiles = S // SCAN_TILE
    tile = pl.BlockSpec((None, SCAN_TILE, REC_WIDTH), lambda b, i: (b, i, 0))
    const = lambda b, i: (0, 0)
    vec = pl.BlockSpec((1, REC_WIDTH), const)
    rider_specs, rider_shapes = _rider_specs(riders, B * n_tiles, lambda b, i: (b * n_tiles + i, 0))
    return pl.pallas_call(
        _rglru_kernel,
        grid=(B, n_tiles),
        in_specs=[tile, tile,
                  pl.BlockSpec((CONV_WIDTH, REC_WIDTH), const), vec,
                  pl.BlockSpec((REC_WIDTH, 2 * REC_WIDTH), const),
                  pl.BlockSpec((1, 2 * REC_WIDTH), const), vec, vec] + rider_specs,
        out_specs=[tile] + rider_specs,
        out_shape=[jax.ShapeDtypeStruct((B, S, REC_WIDTH), BF16)] + rider_shapes,
        scratch_shapes=[pltpu.VMEM((SUBLANES, REC_WIDTH), F32),
                        pltpu.VMEM((SCAN_TILE, REC_WIDTH), F32),
                        pltpu.VMEM((SCAN_TILE, REC_WIDTH), F32),
                        pltpu.VMEM((SUBLANES, REC_WIDTH), F32)],
        compiler_params=_params(("arbitrary", "arbitrary")),
        name="rglru",
    )(u3, gate3, conv_w, conv_b, wab_bf, bab, lam, g_out, *riders)


def _attn_kernel(q_ref, k_ref, v_ref, bias_ref, gout_ref, o_ref):
    S = q_ref.shape[0]
    lane_head = lax.broadcasted_iota(jnp.int32, (Q_TILE, GROUP_LANES), 1) // ATT_HEAD_DIM

    def tile(t, skip):
        r0 = t * Q_TILE
        k0 = r0 - PAD_ROWS + skip
        if not isinstance(t, int):
            r0 = pl.multiple_of(r0, Q_TILE)
            k0 = pl.multiple_of(k0, Q_TILE)
        q_rows = q_ref[pl.ds(r0, Q_TILE), :]
        k_win = k_ref[pl.ds(k0, WINDOW - skip), :]
        v_win = v_ref[pl.ds(k0, WINDOW - skip), :]
        outs = []
        for hg in range(HEAD_GROUPS):
            sl = slice(hg * GROUP_LANES, (hg + 1) * GROUP_LANES)
            q4 = q_rows[:, sl]
            zero = jnp.zeros_like(q4)
            q_stack = jnp.concatenate(
                [jnp.where(lane_head == j, q4, zero) for j in range(GROUP_HEADS)], axis=0)
            s = lax.dot_general(q_stack, k_win[:, sl], (((1,), (1,)), ((), ())),
                                preferred_element_type=F32)
            s = s + bias_ref[hg][:, skip:]
            m = jnp.max(s, axis=-1, keepdims=True)
            e = jnp.exp(s - m)
            denom = jnp.sum(e, axis=-1, keepdims=True)
            o = jnp.dot(e.astype(BF16), v_win[:, sl], preferred_element_type=F32)
            o = o / denom
            out = o[:Q_TILE]
            for j in range(1, GROUP_HEADS):
                out = jnp.where(lane_head == j, o[j * Q_TILE:(j + 1) * Q_TILE], out)
            outs.append(out)
        y = jnp.concatenate(outs, axis=1)
        o_ref[pl.ds(r0, Q_TILE), :] = _rms(y, gout_ref[...]).astype(BF16)

    first_full = PAD_ROWS // Q_TILE
    for t in range(first_full):
        tile(t, PAD_ROWS - t * Q_TILE)

    def full_body(t, carry):
        tile(t, 0)
        return carry

    lax.fori_loop(first_full, S // Q_TILE, full_body, 0, unroll=6)


def _attention(q3, k3, v3, bias_tiles, g_out):
    B, S, _ = q3.shape
    seq = pl.BlockSpec((None, S, ATT_WIDTH), lambda b: (b, 0, 0))
    return pl.pallas_call(
        _attn_kernel,
        grid=(B,),
        in_specs=[seq, seq, seq,
                  pl.BlockSpec((HEAD_GROUPS, GROUP_HEADS * Q_TILE, WINDOW), lambda b: (0, 0, 0)),
                  pl.BlockSpec((1, ATT_WIDTH), lambda b: (0, 0))],
        out_specs=seq,
        out_shape=jax.ShapeDtypeStruct((B, S, ATT_WIDTH), BF16),
        compiler_params=_params(("arbitrary",)),
        name="attn",
    )(q3, k3, v3, bias_tiles, g_out)


def _route(lt):
    n_tok = lt.shape[1]
    row = lambda j: lt[j:j + 1, :]

    gl = [row(j) for j in range(N_GROUPS)]
    gmax = jnp.maximum(jnp.maximum(gl[0], gl[1]), jnp.maximum(gl[2], gl[3]))
    grp = jnp.where(gl[0] >= gmax, 0, jnp.where(gl[1] >= gmax, 1, jnp.where(gl[2] >= gmax, 2, 3)))
    denom = (jnp.exp(gl[0] - gmax) + jnp.exp(gl[1] - gmax)
             + jnp.exp(gl[2] - gmax) + jnp.exp(gl[3] - gmax))
    p_g = 1.0 / denom

    sel = []
    for j in range(EXPERTS_PER_GROUP):
        cand = [row(N_GROUPS + g * EXPERTS_PER_GROUP + j) for g in range(N_GROUPS)]
        sel.append(jnp.where(grp == 0, cand[0],
                             jnp.where(grp == 1, cand[1], jnp.where(grp == 2, cand[2], cand[3]))))

    def first_argmax(vals):
        top = jnp.maximum(jnp.maximum(vals[0], vals[1]), jnp.maximum(vals[2], vals[3]))
        idx = jnp.where(vals[0] >= top, 0,
                        jnp.where(vals[1] >= top, 1, jnp.where(vals[2] >= top, 2, 3)))
        return top, idx

    v1, i1 = first_argmax(sel)
    rest = [jnp.where(i1 == j, -jnp.inf, sel[j]) for j in range(EXPERTS_PER_GROUP)]
    v2, i2 = first_argmax(rest)
    t = jnp.exp(v2 - v1)
    w1 = p_g * (1.0 / (1.0 + t))
    w2 = p_g * (t / (1.0 + t))
    sub = lax.broadcasted_iota(jnp.int32, (ROUTE_ROWS, n_tok), 0)
    table = jnp.zeros((ROUTE_ROWS, n_tok), F32)
    for j in range(EXPERTS_PER_GROUP):
        w_j = jnp.where(i1 == j, w1, 0.0) + jnp.where(i2 == j, w2, 0.0)
        table = jnp.where(sub == j, jnp.broadcast_to(w_j, (ROUTE_ROWS, n_tok)), table)
    return table, grp


def _outproj_kernel(ma_ref, mb_ref, w_ref, x_ref, g2_ref, wr_ref, br_ref,
                    h_ref, xn_ref, route_ref):
    mix = jnp.concatenate([ma_ref[...], mb_ref[...]], axis=1)
    h = x_ref[...] + jnp.dot(mix, w_ref[...], preferred_element_type=F32)
    h_ref[...] = h
    xn = _rms(h, g2_ref[...])
    xn_ref[:, :D_MODEL] = xn

    x_hi = xn.astype(BF16)
    x_lo = (xn - x_hi.astype(F32)).astype(BF16)
    wr = wr_ref[...]
    both = jnp.dot(x_hi, wr, preferred_element_type=F32)
    logits = (both[:, :ROUTE_LANES] + both[:, ROUTE_LANES:]
              + jnp.dot(x_lo, wr[:, :ROUTE_LANES], preferred_element_type=F32)
              + br_ref[...])
    table, grp = _route(logits.T)
    n_tok = route_ref.shape[1]
    sub = lax.broadcasted_iota(jnp.int32, (ROUTE_ROWS, n_tok), 0)
    route_ref[...] = jnp.where(sub == EXPERTS_PER_GROUP,
                               jnp.broadcast_to(grp.astype(F32), (ROUTE_ROWS, n_tok)), table)
    padded = jnp.concatenate([table, jnp.zeros((ROUTE_LANES - ROUTE_ROWS, n_tok), F32)], axis=0)
    xn_ref[:, D_MODEL:] = padded.T


def _outproj(mix_a, mix_b, w_bf, x2, g2, wr_split, br):
    T = x2.shape[0]
    row = lambda i: (i, 0)
    const = lambda i: (0, 0)
    full = pl.BlockSpec((OUT_TILE, D_MODEL), row)
    half = pl.BlockSpec((OUT_TILE, REC_WIDTH), row)
    return pl.pallas_call(
        _outproj_kernel,
        grid=(T // OUT_TILE,),
        in_specs=[half, half,
                  pl.BlockSpec((D_MODEL, D_MODEL), const),
                  full,
                  pl.BlockSpec((1, D_MODEL), const),
                  pl.BlockSpec((D_MODEL, 2 * ROUTE_LANES), const),
                  pl.BlockSpec((1, ROUTE_LANES), const)],
        out_specs=[full, pl.BlockSpec((OUT_TILE, TOKEN_ROW), row),
                   pl.BlockSpec((ROUTE_ROWS, OUT_TILE), lambda i: (0, i))],
        out_shape=[jax.ShapeDtypeStruct((T, D_MODEL), F32),
                   jax.ShapeDtypeStruct((T, TOKEN_ROW), F32),
                   jax.ShapeDtypeStruct((ROUTE_ROWS, T), F32)],
        compiler_params=_params(("arbitrary",)),
        name="outproj",
    )(mix_a, mix_b, w_bf, x2, g2, wr_split, br)


def _experts_kernel(bgroup_ref, nvalid_ref, x_ref, wg_ref, wu_ref, wd_ref, y_ref):
    del bgroup_ref
    active = pl.program_id(0) < nvalid_ref[0]

    @pl.when(active)
    def _():
        rows = x_ref[...]
        x = rows[:, :D_MODEL].astype(BF16)
        ws = rows[:, D_MODEL:]
        hidden = []
        for j in range(EXPERTS_PER_GROUP):
            gate = jnp.dot(x, wg_ref[j], preferred_element_type=F32)
            up = jnp.dot(x, wu_ref[j], preferred_element_type=F32)
            hid = (jax.nn.silu(gate) * up) * ws[:, j:j + 1]
            hidden.append(hid.astype(BF16))
        hcat = jnp.concatenate(hidden, axis=1)
        wd = wd_ref[...].reshape(EXPERTS_PER_GROUP * D_FF_EXPERT, D_MODEL)
        y_ref[...] = jnp.dot(hcat, wd, preferred_element_type=F32)

    @pl.when(jnp.logical_not(active))
    def _():
        y_ref[...] = jnp.zeros(y_ref.shape, F32)


def _experts(bgroup, nvalid, rows_sorted, wg_bf, wu_bf, wd_bf):
    n_pad = rows_sorted.shape[0]
    grp = lambda i, bg, nv: (bg[i], 0, 0)
    row = lambda i, bg, nv: (i, 0)
    grid_spec = pltpu.PrefetchScalarGridSpec(
        num_scalar_prefetch=2,
        grid=(n_pad // EXPERT_BLOCK,),
        in_specs=[pl.BlockSpec((EXPERT_BLOCK, TOKEN_ROW), row),
                  pl.BlockSpec((EXPERTS_PER_GROUP, D_MODEL, D_FF_EXPERT), grp),
                  pl.BlockSpec((EXPERTS_PER_GROUP, D_MODEL, D_FF_EXPERT), grp),
                  pl.BlockSpec((EXPERTS_PER_GROUP, D_FF_EXPERT, D_MODEL), grp)],
        out_specs=pl.BlockSpec((EXPERT_BLOCK, D_MODEL), row),
    )
    return pl.pallas_call(
        _experts_kernel,
        grid_spec=grid_spec,
        out_shape=jax.ShapeDtypeStruct((n_pad, D_MODEL), F32),
        compiler_params=_params(("arbitrary",)),
        name="experts",
    )(bgroup, nvalid, rows_sorted, wg_bf, wu_bf, wd_bf)


def _dispatch_kernel(dest_ref, pads_ref, x_hbm, out_hbm, buf_ref, zero_ref, lsem, ssem, zsem):
    n = MOVE_TILE
    i = pl.program_id(0)
    steps = pl.num_programs(0)

    def load(t, s):
        return pltpu.make_async_copy(x_hbm.at[pl.ds(t * n, n)], buf_ref.at[s], lsem.at[s])

    def scatter(slot_row, r, s):
        return pltpu.make_async_copy(buf_ref.at[s, pl.ds(r, 1)], out_hbm.at[pl.ds(slot_row, 1)],
                                     ssem.at[s])

    def wait_scatters(s):
        pltpu.make_async_copy(buf_ref.at[s], out_hbm.at[pl.ds(0, n)], ssem.at[s]).wait()

    @pl.when(i == 0)
    def _():
        load(0, 0).start()
        load(1, 1).start()

    @pl.when(i >= 2)
    def _():
        wait_scatters((i + 2) % MOVE_RING)

    @pl.when(i + 2 < steps)
    def _():
        load(i + 2, (i + 2) % MOVE_RING).start()

    load(0, i % MOVE_RING).wait()

    def issue(cur):
        for r in range(n):
            scatter(dest_ref[i * n + r], r, cur).start(priority=r % 2)

    lax.switch(i % MOVE_RING, [functools.partial(issue, cur) for cur in range(MOVE_RING)])

    @pl.when(i == steps - 1)
    def _():
        wait_scatters((i + 3) % MOVE_RING)
        wait_scatters(i % MOVE_RING)
        zero_ref[...] = jnp.zeros(zero_ref.shape, F32)

        def fill(row):
            return pltpu.make_async_copy(zero_ref.at[pl.ds(0, 1)], out_hbm.at[pl.ds(row, 1)], zsem)

        def start_fill(row, carry):
            fill(row).start()
            return carry

        def wait_fill(row, carry):
            fill(row).wait()
            return carry

        for phase in (start_fill, wait_fill):
            for k in range(N_GROUPS + 1):
                lax.fori_loop(pads_ref[2 * k], pads_ref[2 * k + 1], phase, 0)


def _dispatch(dest, pads, rows, n_pad):
    T = rows.shape[0]
    assert T // MOVE_TILE >= MOVE_RING
    grid_spec = pltpu.PrefetchScalarGridSpec(
        num_scalar_prefetch=2,
        grid=(T // MOVE_TILE,),
        in_specs=[pl.BlockSpec(memory_space=pl.ANY)],
        out_specs=pl.BlockSpec(memory_space=pl.ANY),
        scratch_shapes=[pltpu.VMEM((MOVE_RING, MOVE_TILE, TOKEN_ROW), F32),
                        pltpu.VMEM((SUBLANES, TOKEN_ROW), F32),
                        pltpu.SemaphoreType.DMA((MOVE_RING,)),
                        pltpu.SemaphoreType.DMA((MOVE_RING,)),
                        pltpu.SemaphoreType.DMA(())],
    )
    return pl.pallas_call(
        _dispatch_kernel,
        grid_spec=grid_spec,
        out_shape=jax.ShapeDtypeStruct((n_pad, TOKEN_ROW), F32),
        compiler_params=_params(("arbitrary",)),
        name="dispatch",
    )(dest, pads, rows)


def _final_kernel(dest_ref, y_hbm, h_ref, g_ref, o_ref, ybuf_ref, gsem):
    n = MOVE_TILE
    i = pl.program_id(0)
    last = pl.num_programs(0) - 1

    def gather(slot_row, r, s):
        return pltpu.make_async_copy(y_hbm.at[pl.ds(slot_row, 1)], ybuf_ref.at[s, pl.ds(r, 1)],
                                     gsem.at[s])

    @pl.when(i == 0)
    def _():
        _for_rows(n, lambda r: gather(dest_ref[r], r, 0).start())
        _for_rows(n, lambda r: gather(dest_ref[n + r], r, 1).start())

    nxt = jnp.minimum(i + 2, last) * n

    def wait_gathers(s):
        pltpu.make_async_copy(y_hbm.at[pl.ds(0, n)], ybuf_ref.at[s], gsem.at[s]).wait()

    wait_gathers(i % RING)

    def step(cur):
        for r in range(n):
            gather(dest_ref[nxt + r], r, (cur + 2) % RING).start(priority=r % 2)
        o_ref[...] = _rms(h_ref[...] + ybuf_ref[cur], g_ref[...])

    lax.switch(i % RING, [functools.partial(step, cur) for cur in range(RING)])

    @pl.when(i == last)
    def _():
        wait_gathers((i + 1) % RING)
        wait_gathers((i + 2) % RING)


def _final(dest, y_sorted, h1, g):
    T = h1.shape[0]
    assert T // MOVE_TILE >= RING
    row = lambda i, dest: (i, 0)
    grid_spec = pltpu.PrefetchScalarGridSpec(
        num_scalar_prefetch=1,
        grid=(T // MOVE_TILE,),
        in_specs=[pl.BlockSpec(memory_space=pl.ANY),
                  pl.BlockSpec((MOVE_TILE, D_MODEL), row),
                  pl.BlockSpec((1, D_MODEL), lambda i, dest: (0, 0))],
        out_specs=pl.BlockSpec((MOVE_TILE, D_MODEL), row),
        scratch_shapes=[pltpu.VMEM((RING, MOVE_TILE, D_MODEL), F32),
                        pltpu.SemaphoreType.DMA((RING,))],
    )
    return pl.pallas_call(
        _final_kernel,
        grid_spec=grid_spec,
        out_shape=jax.ShapeDtypeStruct((T, D_MODEL), F32),
        compiler_params=_params(("arbitrary",)),
        name="final",
    )(dest, y_sorted, h1, g)


def _block_diag(w):
    h, d, _ = w.shape
    eye = jnp.eye(h, dtype=w.dtype)
    return (eye[:, None, :, None] * w[:, :, None, :]).reshape(h * d, h * d)


def _bias_table(rel_bias):
    rb = rel_bias.astype(F32)
    n_far = PAD_ROWS - MAX_REL + CHUNK
    far = jnp.broadcast_to(rb[:, 2 * MAX_REL:], (ATT_HEADS, n_far))
    near = rb[:, MAX_REL - CHUNK + 1:2 * MAX_REL][:, ::-1]
    diag = jnp.concatenate([far, near], axis=1)
    bias = jnp.stack([diag[:, CHUNK - 1 - q:CHUNK - 1 - q + BAND] for q in range(CHUNK)], axis=1)
    off = jnp.full((ATT_HEADS, CHUNK, CHUNK), -1e30, F32)
    first = jnp.concatenate([bias, off], axis=2)
    second = jnp.concatenate([off, bias], axis=2)
    tiles = jnp.concatenate([first, second], axis=1)
    return tiles.reshape(HEAD_GROUPS, GROUP_HEADS * Q_TILE, WINDOW)


def _dispatch_plan(grp, T):
    onehot = (grp[:, None] == jnp.arange(N_GROUPS)[None, :]).astype(jnp.int32)
    incl = jnp.cumsum(onehot, axis=0)
    counts = incl[-1]
    padded = (counts + EXPERT_BLOCK - 1) // EXPERT_BLOCK * EXPERT_BLOCK
    pend = jnp.cumsum(padded)
    pstart = pend - padded
    dest = jnp.sum((incl - onehot + pstart[None, :]) * onehot, axis=1).astype(jnp.int32)
    n_pad = T + N_GROUPS * EXPERT_BLOCK
    n_blocks = n_pad // EXPERT_BLOCK
    used_end = pstart + counts
    pads = jnp.stack([jnp.concatenate([used_end, pend[-1:]]),
                      jnp.concatenate([pend, jnp.full((1,), n_pad, pend.dtype)])], axis=1)
    pads = pads.reshape(-1).astype(jnp.int32)
    bstart = jnp.arange(n_blocks, dtype=jnp.int32) * EXPERT_BLOCK
    bgroup = jnp.sum((pend[None, :] <= bstart[:, None]).astype(jnp.int32), axis=1)
    bgroup = jnp.minimum(bgroup, N_GROUPS - 1)
    nvalid = (pend[-1:] // EXPERT_BLOCK).astype(jnp.int32)
    return dest, pads, bgroup, nvalid, n_pad


def kernel(x, norm1_g, w_in, conv_w, conv_b, w_rg_a, b_rg_a, w_rg_x, b_rg_x, lru_lambda,
           rel_bias, g_rec_out, g_att_out, w_out, norm2_g, w_group, b_group, w_router,
           b_router, w_e_gate, w_e_up, w_e_down, final_g):
    B, S, D = x.shape
    T = B * S
    assert w_in.shape[0] == 1, "single-layer block"
    (norm1_g, w_in, conv_w, conv_b, w_rg_a, b_rg_a, w_rg_x, b_rg_x, lru_lambda, rel_bias, g_rec_out,
     g_att_out, w_out, norm2_g, w_group, b_group, w_router, b_router, w_e_gate, w_e_up, w_e_down) = (
        a.reshape(a.shape[1:]) for a in (
            norm1_g, w_in, conv_w, conv_b, w_rg_a, b_rg_a, w_rg_x, b_rg_x, lru_lambda, rel_bias,
            g_rec_out, g_att_out, w_out, norm2_g, w_group, b_group, w_router, b_router,
            w_e_gate, w_e_up, w_e_down))
    h = x.reshape(T, D)
    n_exp, _, d_ff = w_e_gate.shape
    u, gate, q, k, v, wd_bf, wo_bf = _inproj(
        h, norm1_g.reshape(1, D), w_in, [w_e_down.reshape(n_exp * d_ff, D), w_out])

    wab = jnp.concatenate([_block_diag(w_rg_a), _block_diag(w_rg_x)], axis=1).astype(BF16)
    bab = jnp.concatenate([b_rg_a.reshape(1, REC_WIDTH), b_rg_x.reshape(1, REC_WIDTH)], axis=1)
    mix_a, wg_bf, wu_bf = _rglru(
        u.reshape(B, S, REC_WIDTH), gate.reshape(B, S, REC_WIDTH),
        conv_w, conv_b.reshape(1, REC_WIDTH), wab, bab,
        lru_lambda.reshape(1, REC_WIDTH), g_rec_out.reshape(1, REC_WIDTH),
        [w_e_gate.reshape(n_exp * D, d_ff), w_e_up.reshape(n_exp * D, d_ff)])

    mix_b = _attention(q.reshape(B, S, ATT_WIDTH), k.reshape(B, S, ATT_WIDTH),
                       v.reshape(B, S, ATT_WIDTH), _bias_table(rel_bias),
                       g_att_out.reshape(1, ATT_WIDTH))

    n_route = N_GROUPS + N_GROUPS * EXPERTS_PER_GROUP
    wr = jnp.concatenate([w_group.astype(F32), w_router.astype(F32)], axis=1)
    wr = jnp.pad(wr, ((0, 0), (0, ROUTE_LANES - n_route)))
    wr_hi = wr.astype(BF16)
    wr_lo = (wr - wr_hi.astype(F32)).astype(BF16)
    br = jnp.concatenate([b_group.astype(F32), b_router.astype(F32)])
    br = jnp.pad(br, (0, ROUTE_LANES - n_route)).reshape(1, ROUTE_LANES)
    h1, xn, route = _outproj(mix_a.reshape(T, REC_WIDTH), mix_b.reshape(T, ATT_WIDTH),
                             wo_bf, h, norm2_g.reshape(1, D),
                             jnp.concatenate([wr_hi, wr_lo], axis=1), br)

    grp = route[EXPERTS_PER_GROUP].astype(jnp.int32)
    dest, pads, bgroup, nvalid, n_pad = _dispatch_plan(grp, T)
    rows_sorted = _dispatch(dest, pads, xn, n_pad)
    y_sorted = _experts(bgroup, nvalid, rows_sorted, wg_bf.reshape(n_exp, D, d_ff),
                        wu_bf.reshape(n_exp, D, d_ff), wd_bf.reshape(n_exp, d_ff, D))
    out = _final(dest, y_sorted, h1, final_g.reshape(1, D))
    return out.reshape(B, S, D)
```

```python
import functools

import jax
import jax.numpy as jnp
from jax import lax
from jax.experimental import pallas as pl
from jax.experimental.pallas import tpu as pltpu

F32 = jnp.float32
BF16 = jnp.bfloat16

D_MODEL = 1024
CHUNK = 64
EPS = 1e-6
REC_WIDTH = 512
CONV_WIDTH = 4
LRU_C = 8.0
ATT_HEAD_DIM = 64
ATT_HEADS = 8
ATT_WIDTH = 512
LEFT_CHUNKS = 8
BAND = (LEFT_CHUNKS + 1) * CHUNK
MAX_REL = 128
IN_WIDTH = 2 * REC_WIDTH + 3 * ATT_WIDTH
N_GROUPS = 4
EXPERTS_PER_GROUP = 4
D_FF_EXPERT = 512

SUBLANES = 8
LANES = 128
VMEM_LIMIT_BYTES = 56 * 1024 * 1024

ROW_TILE = 512
OUT_TILE = 1024
SCAN_TILE = 1024
EXPERT_BLOCK = 256
MOVE_TILE = 512
RING = 3
MOVE_RING = 4
Q_TILE = 2 * CHUNK
WINDOW = BAND + CHUNK
GROUP_HEADS = 4
GROUP_LANES = GROUP_HEADS * ATT_HEAD_DIM
HEAD_GROUPS = ATT_HEADS // GROUP_HEADS
PAD_ROWS = LEFT_CHUNKS * CHUNK
ROUTE_LANES = LANES
ROUTE_ROWS = SUBLANES
TOKEN_ROW = D_MODEL + ROUTE_LANES


def _params(semantics, fuse_inputs=None):
    return pltpu.CompilerParams(dimension_semantics=semantics, vmem_limit_bytes=VMEM_LIMIT_BYTES,
                                allow_input_fusion=fuse_inputs)


def _rms(x, g):
    ms = jnp.mean(x * x, axis=-1, keepdims=True)
    return (x * lax.rsqrt(ms + EPS)) * g


def _for_rows(n, fn):
    def body(r, carry):
        fn(r)
        return carry
    lax.fori_loop(0, n, body, 0, unroll=8)


def _inproj_kernel(x_ref, g_ref, w_ref, *rest):
    n = (len(rest) - 6) // 2
    riders_in, outs, riders_out, wbf_ref = rest[:n], rest[n:n + 5], rest[n + 5:2 * n + 5], rest[-1]
    u_ref, gate_ref, q_ref, k_ref, v_ref = outs

    @pl.when(pl.program_id(0) == 0)
    def _():
        wbf_ref[...] = w_ref[...].astype(BF16)

    xn = _rms(x_ref[...], g_ref[...])
    z = jnp.dot(xn.astype(BF16), wbf_ref[...], preferred_element_type=F32)
    u_ref[...] = z[:, :REC_WIDTH]
    gate_ref[...] = z[:, REC_WIDTH:2 * REC_WIDTH]
    o = 2 * REC_WIDTH
    q_ref[...] = (z[:, o:o + ATT_WIDTH] * (ATT_HEAD_DIM ** -0.5)).astype(BF16)
    k_ref[...] = z[:, o + ATT_WIDTH:o + 2 * ATT_WIDTH].astype(BF16)
    v_ref[...] = z[:, o + 2 * ATT_WIDTH:o + 3 * ATT_WIDTH].astype(BF16)
    for src, dst in zip(riders_in, riders_out):
        dst[...] = src[...].astype(BF16)


def _rider_specs(arrays, n_steps, index_map):
    specs = [pl.BlockSpec((a.shape[0] // n_steps, a.shape[1]), index_map) for a in arrays]
    shapes = [jax.ShapeDtypeStruct(a.shape, BF16) for a in arrays]
    return specs, shapes


def _inproj(x2, g, w, riders):
    T = x2.shape[0]
    n_steps = T // ROW_TILE
    row = lambda i: (i, 0)
    const = lambda i: (0, 0)
    half = pl.BlockSpec((ROW_TILE, REC_WIDTH), row)
    rider_specs, rider_shapes = _rider_specs(riders, n_steps, row)
    return pl.pallas_call(
        _inproj_kernel,
        grid=(n_steps,),
        in_specs=[pl.BlockSpec((ROW_TILE, D_MODEL), row),
                  pl.BlockSpec((1, D_MODEL), const),
                  pl.BlockSpec((D_MODEL, IN_WIDTH), const)] + rider_specs,
        out_specs=[half, half, half, half, half] + rider_specs,
        out_shape=[jax.ShapeDtypeStruct((T, REC_WIDTH), F32),
                   jax.ShapeDtypeStruct((T, REC_WIDTH), F32),
                   jax.ShapeDtypeStruct((T, ATT_WIDTH), BF16),
                   jax.ShapeDtypeStruct((T, ATT_WIDTH), BF16),
                   jax.ShapeDtypeStruct((T, ATT_WIDTH), BF16)] + rider_shapes,
        scratch_shapes=[pltpu.VMEM((D_MODEL, IN_WIDTH), BF16)],
        compiler_params=_params(("arbitrary",)),
        name="inproj",
    )(x2, g, w, *riders)


def _rglru_kernel(u_ref, gate_ref, convw_ref, convb_ref, wab_ref, bab_ref, lam_ref, gout_ref, *rest):
    n_riders = (len(rest) - 5) // 2
    riders_in, o_ref = rest[:n_riders], rest[n_riders]
    riders_out = rest[n_riders + 1:2 * n_riders + 1]
    tail_ref, a_ref, b_ref, carry_ref = rest[-4:]
    for src, dst in zip(riders_in, riders_out):
        dst[...] = src[...].astype(BF16)
    n = SCAN_TILE

    @pl.when(pl.program_id(1) == 0)
    def _():
        tail_ref[...] = jnp.zeros((SUBLANES, REC_WIDTH), F32)
        carry_ref[...] = jnp.zeros((SUBLANES, REC_WIDTH), F32)

    groups = n // SUBLANES
    shape3 = (groups, SUBLANES, REC_WIDTH)
    row8 = lax.broadcasted_iota(jnp.int32, shape3, 1)
    u3 = u_ref[...].reshape(shape3)
    ext3 = jnp.concatenate([tail_ref[...].reshape(1, SUBLANES, REC_WIDTH), u3], axis=0)
    tail_ref[...] = u3[groups - 1]
    uc3 = convb_ref[...].reshape(1, 1, REC_WIDTH)
    for j in range(CONV_WIDTH):
        shift = CONV_WIDTH - 1 - j
        if shift:
            rolled = pltpu.roll(ext3, shift, 1)
            x = jnp.where(row8 >= shift, rolled[1:], rolled[:-1])
        else:
            x = u3
        uc3 = uc3 + x * convw_ref[j:j + 1, :].reshape(1, 1, REC_WIDTH)
    uc = uc3.reshape(n, REC_WIDTH)

    ri = jnp.dot(uc.astype(BF16), wab_ref[...], preferred_element_type=F32) + bab_ref[...]
    r = jax.nn.sigmoid(ri[:, :REC_WIDTH])
    ig = jax.nn.sigmoid(ri[:, REC_WIDTH:])
    log_a = (-LRU_C * r) * jax.nn.softplus(-lam_ref[...])
    a = jnp.exp(log_a)
    bx = jnp.sqrt(-jnp.tanh(log_a) * (a * a + 1.0)) * (ig * uc)

    a = a.reshape(shape3)
    bx = bx.reshape(shape3)
    for s in (1, 2, 4):
        keep = row8 >= s
        a_prev = jnp.where(keep, pltpu.roll(a, s, 1), 1.0)
        b_prev = jnp.where(keep, pltpu.roll(bx, s, 1), 0.0)
        bx = a * b_prev + bx
        a = a * a_prev
    a_ref[...] = a.reshape(n, REC_WIDTH)
    b_ref[...] = bx.reshape(n, REC_WIDTH)

    def group(gi, carry):
        r0 = pl.multiple_of(gi * SUBLANES, SUBLANES)
        h = b_ref[pl.ds(r0, SUBLANES), :] + a_ref[pl.ds(r0, SUBLANES), :] * carry
        b_ref[pl.ds(r0, SUBLANES), :] = h
        return jnp.broadcast_to(h[SUBLANES - 1:SUBLANES, :], (SUBLANES, REC_WIDTH))

    carry_ref[...] = lax.fori_loop(0, n // SUBLANES, group, carry_ref[...], unroll=4)

    y = b_ref[...] * jax.nn.gelu(gate_ref[...])
    o_ref[...] = _rms(y, gout_ref[...]).astype(BF16)


def _rglru(u3, gate3, conv_w, conv_b, wab_bf, bab, lam, g_out, riders):
    B, S, _ = u3.shape
    n_tiles = S // SCAN_TILE
    tile = pl.BlockSpec((None, SCAN_TILE, REC_WIDTH), lambda b, i: (b, i, 0))
    const = lambda b, i: (0, 0)
    vec = pl.BlockSpec((1, REC_WIDTH), const)
    rider_specs, rider_shapes = _rider_specs(riders, B * n_tiles, lambda b, i: (b * n_tiles + i, 0))
    return pl.pallas_call(
        _rglru_kernel,
        grid=(B, n_tiles),
        in_specs=[tile, tile,
                  pl.BlockSpec((CONV_WIDTH, REC_WIDTH), const), vec,
                  pl.BlockSpec((REC_WIDTH, 2 * REC_WIDTH), const),
                  pl.BlockSpec((1, 2 * REC_WIDTH), const), vec, vec] + rider_specs,
        out_specs=[tile] + rider_specs,
        out_shape=[jax.ShapeDtypeStruct((B, S, REC_WIDTH), BF16)] + rider_shapes,
        scratch_shapes=[pltpu.VMEM((SUBLANES, REC_WIDTH), F32),
                        pltpu.VMEM((SCAN_TILE, REC_WIDTH), F32),
                        pltpu.VMEM((SCAN_TILE, REC_WIDTH), F32),
                        pltpu.VMEM((SUBLANES, REC_WIDTH), F32)],
        compiler_params=_params(("arbitrary", "arbitrary")),
        name="rglru",
    )(u3, gate3, conv_w, conv_b, wab_bf, bab, lam, g_out, *riders)


def _attn_kernel(q_ref, k_ref, v_ref, bias_ref, gout_ref, o_ref):
    S = q_ref.shape[0]
    lane_head = lax.broadcasted_iota(jnp.int32, (Q_TILE, GROUP_LANES), 1) // ATT_HEAD_DIM

    def tile(t, skip):
        r0 = t * Q_TILE
        k0 = r0 - PAD_ROWS + skip
        if not isinstance(t, int):
            r0 = pl.multiple_of(r0, Q_TILE)
            k0 = pl.multiple_of(k0, Q_TILE)
        q_rows = q_ref[pl.ds(r0, Q_TILE), :]
        k_win = k_ref[pl.ds(k0, WINDOW - skip), :]
        v_win = v_ref[pl.ds(k0, WINDOW - skip), :]
        outs = []
        for hg in range(HEAD_GROUPS):
            sl = slice(hg * GROUP_LANES, (hg + 1) * GROUP_LANES)
            q4 = q_rows[:, sl]
            zero = jnp.zeros_like(q4)
            q_stack = jnp.concatenate(
                [jnp.where(lane_head == j, q4, zero) for j in range(GROUP_HEADS)], axis=0)
            s = lax.dot_general(q_stack, k_win[:, sl], (((1,), (1,)), ((), ())),
                                preferred_element_type=F32)
            s = s + bias_ref[hg][:, skip:]
            m = jnp.max(s, axis=-1, keepdims=True)
            e = jnp.exp(s - m)
            denom = jnp.sum(e, axis=-1, keepdims=True)
            o = jnp.dot(e.astype(BF16), v_win[:, sl], preferred_element_type=F32)
            o = o / denom
            out = o[:Q_TILE]
            for j in range(1, GROUP_HEADS):
                out = jnp.where(lane_head == j, o[j * Q_TILE:(j + 1) * Q_TILE], out)
            outs.append(out)
        y = jnp.concatenate(outs, axis=1)
        o_ref[pl.ds(r0, Q_TILE), :] = _rms(y, gout_ref[...]).astype(BF16)

    first_full = PAD_ROWS // Q_TILE
    for t in range(first_full):
        tile(t, PAD_ROWS - t * Q_TILE)

    def full_body(t, carry):
        tile(t, 0)
        return carry

    lax.fori_loop(first_full, S // Q_TILE, full_body, 0, unroll=6)


def _attention(q3, k3, v3, bias_tiles, g_out):
    B, S, _ = q3.shape
    seq = pl.BlockSpec((None, S, ATT_WIDTH), lambda b: (b, 0, 0))
    return pl.pallas_call(
        _attn_kernel,
        grid=(B,),
        in_specs=[seq, seq, seq,
                  pl.BlockSpec((HEAD_GROUPS, GROUP_HEADS * Q_TILE, WINDOW), lambda b: (0, 0, 0)),
                  pl.BlockSpec((1, ATT_WIDTH), lambda b: (0, 0))],
        out_specs=seq,
        out_shape=jax.ShapeDtypeStruct((B, S, ATT_WIDTH), BF16),
        compiler_params=_params(("arbitrary",), [False, False, False, True, False]),
        name="attn",
    )(q3, k3, v3, bias_tiles, g_out)


def _route(lt):
    n_tok = lt.shape[1]
    row = lambda j: lt[j:j + 1, :]

    gl = [row(j) for j in range(N_GROUPS)]
    gmax = jnp.maximum(jnp.maximum(gl[0], gl[1]), jnp.maximum(gl[2], gl[3]))
    grp = jnp.where(gl[0] >= gmax, 0, jnp.where(gl[1] >= gmax, 1, jnp.where(gl[2] >= gmax, 2, 3)))
    denom = (jnp.exp(gl[0] - gmax) + jnp.exp(gl[1] - gmax)
             + jnp.exp(gl[2] - gmax) + jnp.exp(gl[3] - gmax))
    p_g = 1.0 / denom

    sel = []
    for j in range(EXPERTS_PER_GROUP):
        cand = [row(N_GROUPS + g * EXPERTS_PER_GROUP + j) for g in range(N_GROUPS)]
        sel.append(jnp.where(grp == 0, cand[0],
                             jnp.where(grp == 1, cand[1], jnp.where(grp == 2, cand[2], cand[3]))))

    def first_argmax(vals):
        top = jnp.maximum(jnp.maximum(vals[0], vals[1]), jnp.maximum(vals[2], vals[3]))
        idx = jnp.where(vals[0] >= top, 0,
                        jnp.where(vals[1] >= top, 1, jnp.where(vals[2] >= top, 2, 3)))
        return top, idx

    v1, i1 = first_argmax(sel)
    rest = [jnp.where(i1 == j, -jnp.inf, sel[j]) for j in range(EXPERTS_PER_GROUP)]
    v2, i2 = first_argmax(rest)
    t = jnp.exp(v2 - v1)
    w1 = p_g * (1.0 / (1.0 + t))
    w2 = p_g * (t / (1.0 + t))
    sub = lax.broadcasted_iota(jnp.int32, (ROUTE_ROWS, n_tok), 0)
    table = jnp.zeros((ROUTE_ROWS, n_tok), F32)
    for j in range(EXPERTS_PER_GROUP):
        w_j = jnp.where(i1 == j, w1, 0.0) + jnp.where(i2 == j, w2, 0.0)
        table = jnp.where(sub == j, jnp.broadcast_to(w_j, (ROUTE_ROWS, n_tok)), table)
    return table, grp


def _outproj_kernel(ma_ref, mb_ref, w_ref, x_ref, g2_ref, wr_ref, br_ref,
                    h_ref, xn_ref, route_ref):
    mix = jnp.concatenate([ma_ref[...], mb_ref[...]], axis=1)
    h = x_ref[...] + jnp.dot(mix, w_ref[...], preferred_element_type=F32)
    h_ref[...] = h
    xn = _rms(h, g2_ref[...])
    xn_ref[:, :D_MODEL] = xn

    x_hi = xn.astype(BF16)
    x_lo = (xn - x_hi.astype(F32)).astype(BF16)
    wr = wr_ref[...]
    both = jnp.dot(x_hi, wr, preferred_element_type=F32)
    logits = (both[:, :ROUTE_LANES] + both[:, ROUTE_LANES:]
              + jnp.dot(x_lo, wr[:, :ROUTE_LANES], preferred_element_type=F32)
              + br_ref[...])
    table, grp = _route(logits.T)
    n_tok = route_ref.shape[1]
    sub = lax.broadcasted_iota(jnp.int32, (ROUTE_ROWS, n_tok), 0)
    route_ref[...] = jnp.where(sub == EXPERTS_PER_GROUP,
                               jnp.broadcast_to(grp.astype(F32), (ROUTE_ROWS, n_tok)), table)
    padded = jnp.concatenate([table, jnp.zeros((ROUTE_LANES - ROUTE_ROWS, n_tok), F32)], axis=0)
    xn_ref[:, D_MODEL:] = padded.T


def _outproj(mix_a, mix_b, w_bf, x2, g2, wr_split, br):
    T = x2.shape[0]
    row = lambda i: (i, 0)
    const = lambda i: (0, 0)
    full = pl.BlockSpec((OUT_TILE, D_MODEL), row)
    half = pl.BlockSpec((OUT_TILE, REC_WIDTH), row)
    return pl.pallas_call(
        _outproj_kernel,
        grid=(T // OUT_TILE,),
        in_specs=[half, half,
                  pl.BlockSpec((D_MODEL, D_MODEL), const),
                  full,
                  pl.BlockSpec((1, D_MODEL), const),
                  pl.BlockSpec((D_MODEL, 2 * ROUTE_LANES), const),
                  pl.BlockSpec((1, ROUTE_LANES), const)],
        out_specs=[full, pl.BlockSpec((OUT_TILE, TOKEN_ROW), row),
                   pl.BlockSpec((ROUTE_ROWS, OUT_TILE), lambda i: (0, i))],
        out_shape=[jax.ShapeDtypeStruct((T, D_MODEL), F32),
                   jax.ShapeDtypeStruct((T, TOKEN_ROW), F32),
                   jax.ShapeDtypeStruct((ROUTE_ROWS, T), F32)],
        compiler_params=_params(("arbitrary",), [False, False, False, False, False, True, True]),
        name="outproj",
    )(mix_a, mix_b, w_bf, x2, g2, wr_split, br)


def _experts_kernel(bgroup_ref, nvalid_ref, x_ref, wg_ref, wu_ref, wd_ref, y_ref):
    del bgroup_ref
    active = pl.program_id(0) < nvalid_ref[0]

    @pl.when(active)
    def _():
        rows = x_ref[...]
        x = rows[:, :D_MODEL].astype(BF16)
        ws = rows[:, D_MODEL:]
        hidden = []
        for j in range(EXPERTS_PER_GROUP):
            gate = jnp.dot(x, wg_ref[j], preferred_element_type=F32)
            up = jnp.dot(x, wu_ref[j], preferred_element_type=F32)
            hid = (jax.nn.silu(gate) * up) * ws[:, j:j + 1]
            hidden.append(hid.astype(BF16))
        hcat = jnp.concatenate(hidden, axis=1)
        wd = wd_ref[...].reshape(EXPERTS_PER_GROUP * D_FF_EXPERT, D_MODEL)
        y_ref[...] = jnp.dot(hcat, wd, preferred_element_type=F32)

    @pl.when(jnp.logical_not(active))
    def _():
        y_ref[...] = jnp.zeros(y_ref.shape, F32)


def _experts(bgroup, nvalid, rows_sorted, wg_bf, wu_bf, wd_bf):
    n_pad = rows_sorted.shape[0]
    grp = lambda i, bg, nv: (bg[i], 0, 0)
    row = lambda i, bg, nv: (i, 0)
    grid_spec = pltpu.PrefetchScalarGridSpec(
        num_scalar_prefetch=2,
        grid=(n_pad // EXPERT_BLOCK,),
        in_specs=[pl.BlockSpec((EXPERT_BLOCK, TOKEN_ROW), row),
                  pl.BlockSpec((EXPERTS_PER_GROUP, D_MODEL, D_FF_EXPERT), grp),
                  pl.BlockSpec((EXPERTS_PER_GROUP, D_MODEL, D_FF_EXPERT), grp),
                  pl.BlockSpec((EXPERTS_PER_GROUP, D_FF_EXPERT, D_MODEL), grp)],
        out_specs=pl.BlockSpec((EXPERT_BLOCK, D_MODEL), row),
    )
    return pl.pallas_call(
        _experts_kernel,
        grid_spec=grid_spec,
        out_shape=jax.ShapeDtypeStruct((n_pad, D_MODEL), F32),
        compiler_params=_params(("arbitrary",)),
        name="experts",
    )(bgroup, nvalid, rows_sorted, wg_bf, wu_bf, wd_bf)


def _dispatch_kernel(dest_ref, pads_ref, x_hbm, out_hbm, buf_ref, zero_ref, lsem, ssem, zsem):
    n = MOVE_TILE
    i = pl.program_id(0)
    steps = pl.num_programs(0)

    def load(t, s):
        return pltpu.make_async_copy(x_hbm.at[pl.ds(t * n, n)], buf_ref.at[s], lsem.at[s])

    def scatter(slot_row, r, s):
        return pltpu.make_async_copy(buf_ref.at[s, pl.ds(r, 1)], out_hbm.at[pl.ds(slot_row, 1)],
                                     ssem.at[s])

    def wait_scatters(s):
        pltpu.make_async_copy(buf_ref.at[s], out_hbm.at[pl.ds(0, n)], ssem.at[s]).wait()

    def fill(row):
        return pltpu.make_async_copy(zero_ref.at[pl.ds(0, 1)], out_hbm.at[pl.ds(row, 1)], zsem)

    def start_fill(row, carry):
        fill(row).start()
        return carry

    def wait_fill(row, carry):
        fill(row).wait()
        return carry

    def for_pad_rows(phase):
        for k in range(N_GROUPS + 1):
            lax.fori_loop(pads_ref[2 * k], pads_ref[2 * k + 1], phase, 0)

    @pl.when(i == 0)
    def _():
        load(0, 0).start()
        load(1, 1).start()
        zero_ref[...] = jnp.zeros(zero_ref.shape, F32)
        for_pad_rows(start_fill)

    @pl.when(i >= 2)
    def _():
        wait_scatters((i + 2) % MOVE_RING)

    @pl.when(i + 2 < steps)
    def _():
        load(i + 2, (i + 2) % MOVE_RING).start()

    load(0, i % MOVE_RING).wait()

    def issue(cur):
        for r in range(n):
            scatter(dest_ref[i * n + r], r, cur).start(priority=r % 2)

    lax.switch(i % MOVE_RING, [functools.partial(issue, cur) for cur in range(MOVE_RING)])

    @pl.when(i == steps - 1)
    def _():
        wait_scatters((i + 3) % MOVE_RING)
        wait_scatters(i % MOVE_RING)
        for_pad_rows(wait_fill)


def _dispatch(dest, pads, rows, n_pad):
    T = rows.shape[0]
    assert T // MOVE_TILE >= MOVE_RING
    grid_spec = pltpu.PrefetchScalarGridSpec(
        num_scalar_prefetch=2,
        grid=(T // MOVE_TILE,),
        in_specs=[pl.BlockSpec(memory_space=pl.ANY)],
        out_specs=pl.BlockSpec(memory_space=pl.ANY),
        scratch_shapes=[pltpu.VMEM((MOVE_RING, MOVE_TILE, TOKEN_ROW), F32),
                        pltpu.VMEM((SUBLANES, TOKEN_ROW), F32),
                        pltpu.SemaphoreType.DMA((MOVE_RING,)),
                        pltpu.SemaphoreType.DMA((MOVE_RING,)),
                        pltpu.SemaphoreType.DMA(())],
    )
    return pl.pallas_call(
        _dispatch_kernel,
        grid_spec=grid_spec,
        out_shape=jax.ShapeDtypeStruct((n_pad, TOKEN_ROW), F32),
        compiler_params=_params(("arbitrary",)),
        name="dispatch",
    )(dest, pads, rows)


def _final_kernel(dest_ref, y_hbm, h_ref, g_ref, o_ref, ybuf_ref, gsem):
    n = MOVE_TILE
    i = pl.program_id(0)
    last = pl.num_programs(0) - 1

    def gather(slot_row, r, s):
        return pltpu.make_async_copy(y_hbm.at[pl.ds(slot_row, 1)], ybuf_ref.at[s, pl.ds(r, 1)],
                                     gsem.at[s])

    @pl.when(i == 0)
    def _():
        _for_rows(n, lambda r: gather(dest_ref[r], r, 0).start())
        _for_rows(n, lambda r: gather(dest_ref[n + r], r, 1).start())

    nxt = jnp.minimum(i + 2, last) * n

    def wait_gathers(s):
        pltpu.make_async_copy(y_hbm.at[pl.ds(0, n)], ybuf_ref.at[s], gsem.at[s]).wait()

    wait_gathers(i % RING)

    def step(cur):
        for r in range(n):
            gather(dest_ref[nxt + r], r, (cur + 2) % RING).start(priority=r % 2)
        o_ref[...] = _rms(h_ref[...] + ybuf_ref[cur], g_ref[...])

    lax.switch(i % RING, [functools.partial(step, cur) for cur in range(RING)])

    @pl.when(i == last)
    def _():
        wait_gathers((i + 1) % RING)
        wait_gathers((i + 2) % RING)


def _final(dest, y_sorted, h1, g):
    T = h1.shape[0]
    assert T // MOVE_TILE >= RING
    row = lambda i, dest: (i, 0)
    grid_spec = pltpu.PrefetchScalarGridSpec(
        num_scalar_prefetch=1,
        grid=(T // MOVE_TILE,),
        in_specs=[pl.BlockSpec(memory_space=pl.ANY),
                  pl.BlockSpec((MOVE_TILE, D_MODEL), row),
                  pl.BlockSpec((1, D_MODEL), lambda i, dest: (0, 0))],
        out_specs=pl.BlockSpec((MOVE_TILE, D_MODEL), row),
        scratch_shapes=[pltpu.VMEM((RING, MOVE_TILE, D_MODEL), F32),
                        pltpu.SemaphoreType.DMA((RING,))],
    )
    return pl.pallas_call(
        _final_kernel,
        grid_spec=grid_spec,
        out_shape=jax.ShapeDtypeStruct((T, D_MODEL), F32),
        compiler_params=_params(("arbitrary",)),
        name="final",
    )(dest, y_sorted, h1, g)


def _block_diag(w):
    h, d, _ = w.shape
    eye = jnp.eye(h, dtype=w.dtype)
    return (eye[:, None, :, None] * w[:, :, None, :]).reshape(h * d, h * d)


def _bias_table(rel_bias):
    rb = rel_bias.astype(F32)
    n_far = PAD_ROWS - MAX_REL + CHUNK
    far = jnp.broadcast_to(rb[:, 2 * MAX_REL:], (ATT_HEADS, n_far))
    near = rb[:, MAX_REL - CHUNK + 1:2 * MAX_REL][:, ::-1]
    diag = jnp.concatenate([far, near], axis=1)
    bias = jnp.stack([diag[:, CHUNK - 1 - q:CHUNK - 1 - q + BAND] for q in range(CHUNK)], axis=1)
    off = jnp.full((ATT_HEADS, CHUNK, CHUNK), -1e30, F32)
    first = jnp.concatenate([bias, off], axis=2)
    second = jnp.concatenate([off, bias], axis=2)
    tiles = jnp.concatenate([first, second], axis=1)
    return tiles.reshape(HEAD_GROUPS, GROUP_HEADS * Q_TILE, WINDOW)


def _dispatch_plan(grp, T):
    onehot = (grp[:, None] == jnp.arange(N_GROUPS)[None, :]).astype(jnp.int32)
    incl = jnp.cumsum(onehot, axis=0)
    counts = incl[-1]
    padded = (counts + EXPERT_BLOCK - 1) // EXPERT_BLOCK * EXPERT_BLOCK
    pend = jnp.cumsum(padded)
    pstart = pend - padded
    dest = jnp.sum((incl - onehot + pstart[None, :]) * onehot, axis=1).astype(jnp.int32)
    n_pad = T + N_GROUPS * EXPERT_BLOCK
    n_blocks = n_pad // EXPERT_BLOCK
    used_end = pstart + counts
    pads = jnp.stack([jnp.concatenate([used_end, pend[-1:]]),
                      jnp.concatenate([pend, jnp.full((1,), n_pad, pend.dtype)])], axis=1)
    pads = pads.reshape(-1).astype(jnp.int32)
    bstart = jnp.arange(n_blocks, dtype=jnp.int32) * EXPERT_BLOCK
    bgroup = jnp.sum((pend[None, :] <= bstart[:, None]).astype(jnp.int32), axis=1)
    bgroup = jnp.minimum(bgroup, N_GROUPS - 1)
    nvalid = (pend[-1:] // EXPERT_BLOCK).astype(jnp.int32)
    return dest, pads, bgroup, nvalid, n_pad


def kernel(x, norm1_g, w_in, conv_w, conv_b, w_rg_a, b_rg_a, w_rg_x, b_rg_x, lru_lambda,
           rel_bias, g_rec_out, g_att_out, w_out, norm2_g, w_group, b_group, w_router,
           b_router, w_e_gate, w_e_up, w_e_down, final_g):
    B, S, D = x.shape
    T = B * S
    assert w_in.shape[0] == 1, "single-layer block"
    (norm1_g, w_in, conv_w, conv_b, w_rg_a, b_rg_a, w_rg_x, b_rg_x, lru_lambda, rel_bias, g_rec_out,
     g_att_out, w_out, norm2_g, w_group, b_group, w_router, b_router, w_e_gate, w_e_up, w_e_down) = (
        a.reshape(a.shape[1:]) for a in (
            norm1_g, w_in, conv_w, conv_b, w_rg_a, b_rg_a, w_rg_x, b_rg_x, lru_lambda, rel_bias,
            g_rec_out, g_att_out, w_out, norm2_g, w_group, b_group, w_router, b_router,
            w_e_gate, w_e_up, w_e_down))
    h = x.reshape(T, D)
    n_exp, _, d_ff = w_e_gate.shape
    u, gate, q, k, v, wd_bf, wo_bf = _inproj(
        h, norm1_g.reshape(1, D), w_in, [w_e_down.reshape(n_exp * d_ff, D), w_out])

    wab = jnp.concatenate([_block_diag(w_rg_a), _block_diag(w_rg_x)], axis=1).astype(BF16)
    bab = jnp.concatenate([b_rg_a.reshape(1, REC_WIDTH), b_rg_x.reshape(1, REC_WIDTH)], axis=1)
    mix_a, wg_bf, wu_bf = _rglru(
        u.reshape(B, S, REC_WIDTH), gate.reshape(B, S, REC_WIDTH),
        conv_w, conv_b.reshape(1, REC_WIDTH), wab, bab,
        lru_lambda.reshape(1, REC_WIDTH), g_rec_out.reshape(1, REC_WIDTH),
        [w_e_gate.reshape(n_exp * D, d_ff), w_e_up.reshape(n_exp * D, d_ff)])

    mix_b = _attention(q.reshape(B, S, ATT_WIDTH), k.reshape(B, S, ATT_WIDTH),
                       v.reshape(B, S, ATT_WIDTH), _bias_table(rel_bias),
                       g_att_out.reshape(1, ATT_WIDTH))

    n_route = N_GROUPS + N_GROUPS * EXPERTS_PER_GROUP
    wr = jnp.concatenate([w_group.astype(F32), w_router.astype(F32)], axis=1)
    wr = jnp.pad(wr, ((0, 0), (0, ROUTE_LANES - n_route)))
    wr_hi = wr.astype(BF16)
    wr_lo = (wr - wr_hi.astype(F32)).astype(BF16)
    br = jnp.concatenate([b_group.astype(F32), b_router.astype(F32)])
    br = jnp.pad(br, (0, ROUTE_LANES - n_route)).reshape(1, ROUTE_LANES)
    h1, xn, route = _outproj(mix_a.reshape(T, REC_WIDTH), mix_b.reshape(T, ATT_WIDTH),
                             wo_bf, h, norm2_g.reshape(1, D),
                             jnp.concatenate([wr_hi, wr_lo], axis=1), br)

    grp = route[EXPERTS_PER_GROUP].astype(jnp.int32)
    dest, pads, bgroup, nvalid, n_pad = _dispatch_plan(grp, T)
    rows_sorted = _dispatch(dest, pads, xn, n_pad)
    y_sorted = _experts(bgroup, nvalid, rows_sorted, wg_bf.reshape(n_exp, D, d_ff),
                        wu_bf.reshape(n_exp, D, d_ff), wd_bf.reshape(n_exp, d_ff, D))
    out = _final(dest, y_sorted, h1, final_g.reshape(1, D))
    return out.reshape(B, S, D)
```

```python
import functools

import jax
import jax.numpy as jnp
from jax import lax
from jax.experimental import pallas as pl
from jax.experimental.pallas import tpu as pltpu

F32 = jnp.float32
BF16 = jnp.bfloat16

D_MODEL = 1024
CHUNK = 64
EPS = 1e-6
REC_WIDTH = 512
CONV_WIDTH = 4
LRU_C = 8.0
ATT_HEAD_DIM = 64
ATT_HEADS = 8
ATT_WIDTH = 512
LEFT_CHUNKS = 8
BAND = (LEFT_CHUNKS + 1) * CHUNK
MAX_REL = 128
IN_WIDTH = 2 * REC_WIDTH + 3 * ATT_WIDTH
N_GROUPS = 4
EXPERTS_PER_GROUP = 4
D_FF_EXPERT = 512

SUBLANES = 8
LANES = 128
VMEM_LIMIT_BYTES = 56 * 1024 * 1024

ROW_TILE = 512
OUT_TILE = 1024
SCAN_TILE = 1024
EXPERT_BLOCK = 256
MOVE_TILE = 512
RING = 3
MOVE_RING = 4
Q_TILE = 2 * CHUNK
WINDOW = BAND + CHUNK
GROUP_HEADS = 4
GROUP_LANES = GROUP_HEADS * ATT_HEAD_DIM
HEAD_GROUPS = ATT_HEADS // GROUP_HEADS
PAD_ROWS = LEFT_CHUNKS * CHUNK
ROUTE_LANES = LANES
ROUTE_ROWS = SUBLANES
TOKEN_ROW = D_MODEL + ROUTE_LANES


def _params(semantics, fuse_inputs=None):
    return pltpu.CompilerParams(dimension_semantics=semantics, vmem_limit_bytes=VMEM_LIMIT_BYTES,
                                allow_input_fusion=fuse_inputs)


def _rms(x, g):
    ms = jnp.mean(x * x, axis=-1, keepdims=True)
    return (x * lax.rsqrt(ms + EPS)) * g


def _for_rows(n, fn):
    def body(r, carry):
        fn(r)
        return carry
    lax.fori_loop(0, n, body, 0, unroll=8)


def _inproj_kernel(x_ref, g_ref, w_ref, *rest):
    n = (len(rest) - 6) // 2
    riders_in, outs, riders_out, wbf_ref = rest[:n], rest[n:n + 5], rest[n + 5:2 * n + 5], rest[-1]
    u_ref, gate_ref, q_ref, k_ref, v_ref = outs

    @pl.when(pl.program_id(0) == 0)
    def _():
        wbf_ref[...] = w_ref[...].astype(BF16)

    xn = _rms(x_ref[...], g_ref[...])
    z = jnp.dot(xn.astype(BF16), wbf_ref[...], preferred_element_type=F32)
    u_ref[...] = z[:, :REC_WIDTH]
    gate_ref[...] = z[:, REC_WIDTH:2 * REC_WIDTH]
    o = 2 * REC_WIDTH
    q_ref[...] = (z[:, o:o + ATT_WIDTH] * (ATT_HEAD_DIM ** -0.5)).astype(BF16)
    k_ref[...] = z[:, o + ATT_WIDTH:o + 2 * ATT_WIDTH].astype(BF16)
    v_ref[...] = z[:, o + 2 * ATT_WIDTH:o + 3 * ATT_WIDTH].astype(BF16)
    for src, dst in zip(riders_in, riders_out):
        dst[...] = src[...].astype(BF16)


def _rider_specs(arrays, n_steps, index_map):
    specs = [pl.BlockSpec((a.shape[0] // n_steps, a.shape[1]), index_map) for a in arrays]
    shapes = [jax.ShapeDtypeStruct(a.shape, BF16) for a in arrays]
    return specs, shapes


def _inproj(x2, g, w, riders):
    T = x2.shape[0]
    n_steps = T // ROW_TILE
    row = lambda i: (i, 0)
    const = lambda i: (0, 0)
    half = pl.BlockSpec((ROW_TILE, REC_WIDTH), row)
    rider_specs, rider_shapes = _rider_specs(riders, n_steps, row)
    return pl.pallas_call(
        _inproj_kernel,
        grid=(n_steps,),
        in_specs=[pl.BlockSpec((ROW_TILE, D_MODEL), row),
                  pl.BlockSpec((1, D_MODEL), const),
                  pl.BlockSpec((D_MODEL, IN_WIDTH), const)] + rider_specs,
        out_specs=[half, half, half, half, half] + rider_specs,
        out_shape=[jax.ShapeDtypeStruct((T, REC_WIDTH), F32),
                   jax.ShapeDtypeStruct((T, REC_WIDTH), F32),
                   jax.ShapeDtypeStruct((T, ATT_WIDTH), BF16),
                   jax.ShapeDtypeStruct((T, ATT_WIDTH), BF16),
                   jax.ShapeDtypeStruct((T, ATT_WIDTH), BF16)] + rider_shapes,
        scratch_shapes=[pltpu.VMEM((D_MODEL, IN_WIDTH), BF16)],
        compiler_params=_params(("arbitrary",)),
        name="inproj",
    )(x2, g, w, *riders)


def _rglru_kernel(u_ref, gate_ref, convw_ref, convb_ref, wab_ref, bab_ref, lam_ref, gout_ref, *rest):
    n_riders = (len(rest) - 5) // 2
    riders_in, o_ref = rest[:n_riders], rest[n_riders]
    riders_out = rest[n_riders + 1:2 * n_riders + 1]
    tail_ref, a_ref, b_ref, carry_ref = rest[-4:]
    for src, dst in zip(riders_in, riders_out):
        dst[...] = src[...].astype(BF16)
    n = SCAN_TILE

    @pl.when(pl.program_id(1) == 0)
    def _():
        tail_ref[...] = jnp.zeros((SUBLANES, REC_WIDTH), F32)
        carry_ref[...] = jnp.zeros((SUBLANES, REC_WIDTH), F32)

    groups = n // SUBLANES
    shape3 = (groups, SUBLANES, REC_WIDTH)
    row8 = lax.broadcasted_iota(jnp.int32, shape3, 1)
    u3 = u_ref[...].reshape(shape3)
    ext3 = jnp.concatenate([tail_ref[...].reshape(1, SUBLANES, REC_WIDTH), u3], axis=0)
    tail_ref[...] = u3[groups - 1]
    uc3 = convb_ref[...].reshape(1, 1, REC_WIDTH)
    for j in range(CONV_WIDTH):
        shift = CONV_WIDTH - 1 - j
        if shift:
            rolled = pltpu.roll(ext3, shift, 1)
            x = jnp.where(row8 >= shift, rolled[1:], rolled[:-1])
        else:
            x = u3
        uc3 = uc3 + x * convw_ref[j:j + 1, :].reshape(1, 1, REC_WIDTH)
    uc = uc3.reshape(n, REC_WIDTH)

    ri = jnp.dot(uc.astype(BF16), wab_ref[...], preferred_element_type=F32) + bab_ref[...]
    r = jax.nn.sigmoid(ri[:, :REC_WIDTH])
    ig = jax.nn.sigmoid(ri[:, REC_WIDTH:])
    log_a = (-LRU_C * r) * jax.nn.softplus(-lam_ref[...])
    a = jnp.exp(log_a)
    bx = jnp.sqrt(-jnp.tanh(log_a) * (a * a + 1.0)) * (ig * uc)

    a = a.reshape(shape3)
    bx = bx.reshape(shape3)
    for s in (1, 2, 4):
        keep = row8 >= s
        a_prev = jnp.where(keep, pltpu.roll(a, s, 1), 1.0)
        b_prev = jnp.where(keep, pltpu.roll(bx, s, 1), 0.0)
        bx = a * b_prev + bx
        a = a * a_prev
    a_ref[...] = a.reshape(n, REC_WIDTH)
    b_ref[...] = bx.reshape(n, REC_WIDTH)

    def group(gi, carry):
        r0 = pl.multiple_of(gi * SUBLANES, SUBLANES)
        h = b_ref[pl.ds(r0, SUBLANES), :] + a_ref[pl.ds(r0, SUBLANES), :] * carry
        b_ref[pl.ds(r0, SUBLANES), :] = h
        return jnp.broadcast_to(h[SUBLANES - 1:SUBLANES, :], (SUBLANES, REC_WIDTH))

    carry_ref[...] = lax.fori_loop(0, n // SUBLANES, group, carry_ref[...], unroll=4)

    y = b_ref[...] * jax.nn.gelu(gate_ref[...])
    o_ref[...] = _rms(y, gout_ref[...]).astype(BF16)


def _rglru(u3, gate3, conv_w, conv_b, wab_bf, bab, lam, g_out, riders):
    B, S, _ = u3.shape
    n_tiles = S // SCAN_TILE
    tile = pl.BlockSpec((None, SCAN_TILE, REC_WIDTH), lambda b, i: (b, i, 0))
    const = lambda b, i: (0, 0)
    vec = pl.BlockSpec((1, REC_WIDTH), const)
    rider_specs, rider_shapes = _rider_specs(riders, B * n_tiles, lambda b, i: (b * n_tiles + i, 0))
    return pl.pallas_call(
        _rglru_kernel,
        grid=(B, n_tiles),
        in_specs=[tile, tile,
                  pl.BlockSpec((CONV_WIDTH, REC_WIDTH), const), vec,
                  pl.BlockSpec((REC_WIDTH, 2 * REC_WIDTH), const),
                  pl.BlockSpec((1, 2 * REC_WIDTH), const), vec, vec] + rider_specs,
        out_specs=[tile] + rider_specs,
        out_shape=[jax.ShapeDtypeStruct((B, S, REC_WIDTH), BF16)] + rider_shapes,
        scratch_shapes=[pltpu.VMEM((SUBLANES, REC_WIDTH), F32),
                        pltpu.VMEM((SCAN_TILE, REC_WIDTH), F32),
                        pltpu.VMEM((SCAN_TILE, REC_WIDTH), F32),
                        pltpu.VMEM((SUBLANES, REC_WIDTH), F32)],
        compiler_params=_params(("arbitrary", "arbitrary"),
                                [False, False, False, False, True, True] + [False] * (2 + len(riders))),
        name="rglru",
    )(u3, gate3, conv_w, conv_b, wab_bf, bab, lam, g_out, *riders)


def _attn_kernel(q_ref, k_ref, v_ref, bias_ref, gout_ref, o_ref):
    S = q_ref.shape[0]
    lane_head = lax.broadcasted_iota(jnp.int32, (Q_TILE, GROUP_LANES), 1) // ATT_HEAD_DIM

    def tile(t, skip):
        r0 = t * Q_TILE
        k0 = r0 - PAD_ROWS + skip
        if not isinstance(t, int):
            r0 = pl.multiple_of(r0, Q_TILE)
            k0 = pl.multiple_of(k0, Q_TILE)
        q_rows = q_ref[pl.ds(r0, Q_TILE), :]
        k_win = k_ref[pl.ds(k0, WINDOW - skip), :]
        v_win = v_ref[pl.ds(k0, WINDOW - skip), :]
        outs = []
        for hg in range(HEAD_GROUPS):
            sl = slice(hg * GROUP_LANES, (hg + 1) * GROUP_LANES)
            q4 = q_rows[:, sl]
            zero = jnp.zeros_like(q4)
            q_stack = jnp.concatenate(
                [jnp.where(lane_head == j, q4, zero) for j in range(GROUP_HEADS)], axis=0)
            s = lax.dot_general(q_stack, k_win[:, sl], (((1,), (1,)), ((), ())),
                                preferred_element_type=F32)
            s = s + bias_ref[hg][:, skip:]
            m = jnp.max(s, axis=-1, keepdims=True)
            e = jnp.exp(s - m)
            denom = jnp.sum(e, axis=-1, keepdims=True)
            o = jnp.dot(e.astype(BF16), v_win[:, sl], preferred_element_type=F32)
            o = o / denom
            out = o[:Q_TILE]
            for j in range(1, GROUP_HEADS):
                out = jnp.where(lane_head == j, o[j * Q_TILE:(j + 1) * Q_TILE], out)
            outs.append(out)
        y = jnp.concatenate(outs, axis=1)
        o_ref[pl.ds(r0, Q_TILE), :] = _rms(y, gout_ref[...]).astype(BF16)

    first_full = PAD_ROWS // Q_TILE
    for t in range(first_full):
        tile(t, PAD_ROWS - t * Q_TILE)

    def full_body(t, carry):
        tile(t, 0)
        return carry

    lax.fori_loop(first_full, S // Q_TILE, full_body, 0, unroll=6)


def _attention(q3, k3, v3, bias_tiles, g_out):
    B, S, _ = q3.shape
    seq = pl.BlockSpec((None, S, ATT_WIDTH), lambda b: (b, 0, 0))
    return pl.pallas_call(
        _attn_kernel,
        grid=(B,),
        in_specs=[seq, seq, seq,
                  pl.BlockSpec((HEAD_GROUPS, GROUP_HEADS * Q_TILE, WINDOW), lambda b: (0, 0, 0)),
                  pl.BlockSpec((1, ATT_WIDTH), lambda b: (0, 0))],
        out_specs=seq,
        out_shape=jax.ShapeDtypeStruct((B, S, ATT_WIDTH), BF16),
        compiler_params=_params(("arbitrary",), [False, False, False, True, False]),
        name="attn",
    )(q3, k3, v3, bias_tiles, g_out)


def _route(lt):
    n_tok = lt.shape[1]
    row = lambda j: lt[j:j + 1, :]

    gl = [row(j) for j in range(N_GROUPS)]
    gmax = jnp.maximum(jnp.maximum(gl[0], gl[1]), jnp.maximum(gl[2], gl[3]))
    grp = jnp.where(gl[0] >= gmax, 0, jnp.where(gl[1] >= gmax, 1, jnp.where(gl[2] >= gmax, 2, 3)))
    denom = (jnp.exp(gl[0] - gmax) + jnp.exp(gl[1] - gmax)
             + jnp.exp(gl[2] - gmax) + jnp.exp(gl[3] - gmax))
    p_g = 1.0 / denom

    sel = []
    for j in range(EXPERTS_PER_GROUP):
        cand = [row(N_GROUPS + g * EXPERTS_PER_GROUP + j) for g in range(N_GROUPS)]
        sel.append(jnp.where(grp == 0, cand[0],
                             jnp.where(grp == 1, cand[1], jnp.where(grp == 2, cand[2], cand[3]))))

    def first_argmax(vals):
        top = jnp.maximum(jnp.maximum(vals[0], vals[1]), jnp.maximum(vals[2], vals[3]))
        idx = jnp.where(vals[0] >= top, 0,
                        jnp.where(vals[1] >= top, 1, jnp.where(vals[2] >= top, 2, 3)))
        return top, idx

    v1, i1 = first_argmax(sel)
    rest = [jnp.where(i1 == j, -jnp.inf, sel[j]) for j in range(EXPERTS_PER_GROUP)]
    v2, i2 = first_argmax(rest)
    t = jnp.exp(v2 - v1)
    w1 = p_g * (1.0 / (1.0 + t))
    w2 = p_g * (t / (1.0 + t))
    sub = lax.broadcasted_iota(jnp.int32, (ROUTE_ROWS, n_tok), 0)
    table = jnp.zeros((ROUTE_ROWS, n_tok), F32)
    for j in range(EXPERTS_PER_GROUP):
        w_j = jnp.where(i1 == j, w1, 0.0) + jnp.where(i2 == j, w2, 0.0)
        table = jnp.where(sub == j, jnp.broadcast_to(w_j, (ROUTE_ROWS, n_tok)), table)
    return table, grp


def _outproj_kernel(ma_ref, mb_ref, w_ref, x_ref, g2_ref, wr_ref, br_ref,
                    h_ref, xn_ref, route_ref):
    mix = jnp.concatenate([ma_ref[...], mb_ref[...]], axis=1)
    h = x_ref[...] + jnp.dot(mix, w_ref[...], preferred_element_type=F32)
    h_ref[...] = h
    xn = _rms(h, g2_ref[...])
    xn_ref[:, :D_MODEL] = xn

    x_hi = xn.astype(BF16)
    x_lo = (xn - x_hi.astype(F32)).astype(BF16)
    wr = wr_ref[...]
    both = jnp.dot(x_hi, wr, preferred_element_type=F32)
    logits = (both[:, :ROUTE_LANES] + both[:, ROUTE_LANES:]
              + jnp.dot(x_lo, wr[:, :ROUTE_LANES], preferred_element_type=F32)
              + br_ref[...])
    table, grp = _route(logits.T)
    n_tok = route_ref.shape[1]
    sub = lax.broadcasted_iota(jnp.int32, (ROUTE_ROWS, n_tok), 0)
    route_ref[...] = jnp.where(sub == EXPERTS_PER_GROUP,
                               jnp.broadcast_to(grp.astype(F32), (ROUTE_ROWS, n_tok)), table)
    padded = jnp.concatenate([table, jnp.zeros((ROUTE_LANES - ROUTE_ROWS, n_tok), F32)], axis=0)
    xn_ref[:, D_MODEL:] = padded.T


def _outproj(mix_a, mix_b, w_bf, x2, g2, wr_split, br):
    T = x2.shape[0]
    row = lambda i: (i, 0)
    const = lambda i: (0, 0)
    full = pl.BlockSpec((OUT_TILE, D_MODEL), row)
    half = pl.BlockSpec((OUT_TILE, REC_WIDTH), row)
    return pl.pallas_call(
        _outproj_kernel,
        grid=(T // OUT_TILE,),
        in_specs=[half, half,
                  pl.BlockSpec((D_MODEL, D_MODEL), const),
                  full,
                  pl.BlockSpec((1, D_MODEL), const),
                  pl.BlockSpec((D_MODEL, 2 * ROUTE_LANES), const),
                  pl.BlockSpec((1, ROUTE_LANES), const)],
        out_specs=[full, pl.BlockSpec((OUT_TILE, TOKEN_ROW), row),
                   pl.BlockSpec((ROUTE_ROWS, OUT_TILE), lambda i: (0, i))],
        out_shape=[jax.ShapeDtypeStruct((T, D_MODEL), F32),
                   jax.ShapeDtypeStruct((T, TOKEN_ROW), F32),
                   jax.ShapeDtypeStruct((ROUTE_ROWS, T), F32)],
        compiler_params=_params(("arbitrary",), [False, False, False, False, False, True, True]),
        name="outproj",
    )(mix_a, mix_b, w_bf, x2, g2, wr_split, br)


def _experts_kernel(bgroup_ref, nvalid_ref, x_ref, wg_ref, wu_ref, wd_ref, y_ref):
    del bgroup_ref
    active = pl.program_id(0) < nvalid_ref[0]

    @pl.when(active)
    def _():
        rows = x_ref[...]
        x = rows[:, :D_MODEL].astype(BF16)
        ws = rows[:, D_MODEL:]
        hidden = []
        for j in range(EXPERTS_PER_GROUP):
            gate = jnp.dot(x, wg_ref[j], preferred_element_type=F32)
            up = jnp.dot(x, wu_ref[j], preferred_element_type=F32)
            hid = (jax.nn.silu(gate) * up) * ws[:, j:j + 1]
            hidden.append(hid.astype(BF16))
        hcat = jnp.concatenate(hidden, axis=1)
        wd = wd_ref[...].reshape(EXPERTS_PER_GROUP * D_FF_EXPERT, D_MODEL)
        y_ref[...] = jnp.dot(hcat, wd, preferred_element_type=F32)

    @pl.when(jnp.logical_not(active))
    def _():
        y_ref[...] = jnp.zeros(y_ref.shape, F32)


def _experts(bgroup, nvalid, rows_sorted, wg_bf, wu_bf, wd_bf):
    n_pad = rows_sorted.shape[0]
    grp = lambda i, bg, nv: (bg[i], 0, 0)
    row = lambda i, bg, nv: (i, 0)
    grid_spec = pltpu.PrefetchScalarGridSpec(
        num_scalar_prefetch=2,
        grid=(n_pad // EXPERT_BLOCK,),
        in_specs=[pl.BlockSpec((EXPERT_BLOCK, TOKEN_ROW), row),
                  pl.BlockSpec((EXPERTS_PER_GROUP, D_MODEL, D_FF_EXPERT), grp),
                  pl.BlockSpec((EXPERTS_PER_GROUP, D_MODEL, D_FF_EXPERT), grp),
                  pl.BlockSpec((EXPERTS_PER_GROUP, D_FF_EXPERT, D_MODEL), grp)],
        out_specs=pl.BlockSpec((EXPERT_BLOCK, D_MODEL), row),
    )
    return pl.pallas_call(
        _experts_kernel,
        grid_spec=grid_spec,
        out_shape=jax.ShapeDtypeStruct((n_pad, D_MODEL), F32),
        compiler_params=_params(("arbitrary",)),
        name="experts",
    )(bgroup, nvalid, rows_sorted, wg_bf, wu_bf, wd_bf)


def _dispatch_kernel(dest_ref, pads_ref, x_hbm, out_hbm, buf_ref, zero_ref, lsem, ssem, zsem):
    n = MOVE_TILE
    i = pl.program_id(0)
    steps = pl.num_programs(0)

    def load(t, s):
        return pltpu.make_async_copy(x_hbm.at[pl.ds(t * n, n)], buf_ref.at[s], lsem.at[s])

    def scatter(slot_row, r, s):
        return pltpu.make_async_copy(buf_ref.at[s, pl.ds(r, 1)], out_hbm.at[pl.ds(slot_row, 1)],
                                     ssem.at[s])

    def wait_scatters(s):
        pltpu.make_async_copy(buf_ref.at[s], out_hbm.at[pl.ds(0, n)], ssem.at[s]).wait()

    def fill(row):
        return pltpu.make_async_copy(zero_ref.at[pl.ds(0, 1)], out_hbm.at[pl.ds(row, 1)], zsem)

    def start_fill(row, carry):
        fill(row).start()
        return carry

    def wait_fill(row, carry):
        fill(row).wait()
        return carry

    def for_pad_rows(phase):
        for k in range(N_GROUPS + 1):
            lax.fori_loop(pads_ref[2 * k], pads_ref[2 * k + 1], phase, 0)

    @pl.when(i == 0)
    def _():
        load(0, 0).start()
        load(1, 1).start()
        zero_ref[...] = jnp.zeros(zero_ref.shape, F32)
        for_pad_rows(start_fill)

    @pl.when(i >= 2)
    def _():
        wait_scatters((i + 2) % MOVE_RING)

    @pl.when(i + 2 < steps)
    def _():
        load(i + 2, (i + 2) % MOVE_RING).start()

    load(0, i % MOVE_RING).wait()

    def issue(cur):
        for r in range(n):
            scatter(dest_ref[i * n + r], r, cur).start(priority=r % 2)

    lax.switch(i % MOVE_RING, [functools.partial(issue, cur) for cur in range(MOVE_RING)])

    @pl.when(i == steps - 1)
    def _():
        wait_scatters((i + 3) % MOVE_RING)
        wait_scatters(i % MOVE_RING)
        for_pad_rows(wait_fill)


def _dispatch(dest, pads, rows, n_pad):
    T = rows.shape[0]
    assert T // MOVE_TILE >= MOVE_RING
    grid_spec = pltpu.PrefetchScalarGridSpec(
        num_scalar_prefetch=2,
        grid=(T // MOVE_TILE,),
        in_specs=[pl.BlockSpec(memory_space=pl.ANY)],
        out_specs=pl.BlockSpec(memory_space=pl.ANY),
        scratch_shapes=[pltpu.VMEM((MOVE_RING, MOVE_TILE, TOKEN_ROW), F32),
                        pltpu.VMEM((SUBLANES, TOKEN_ROW), F32),
                        pltpu.SemaphoreType.DMA((MOVE_RING,)),
                        pltpu.SemaphoreType.DMA((MOVE_RING,)),
                        pltpu.SemaphoreType.DMA(())],
    )
    return pl.pallas_call(
        _dispatch_kernel,
        grid_spec=grid_spec,
        out_shape=jax.ShapeDtypeStruct((n_pad, TOKEN_ROW), F32),
        compiler_params=_params(("arbitrary",)),
        name="dispatch",
    )(dest, pads, rows)


def _final_kernel(dest_ref, y_hbm, h_ref, g_ref, o_ref, ybuf_ref, gsem):
    n = MOVE_TILE
    i = pl.program_id(0)
    last = pl.num_programs(0) - 1

    def gather(slot_row, r, s):
        return pltpu.make_async_copy(y_hbm.at[pl.ds(slot_row, 1)], ybuf_ref.at[s, pl.ds(r, 1)],
                                     gsem.at[s])

    @pl.when(i == 0)
    def _():
        _for_rows(n, lambda r: gather(dest_ref[r], r, 0).start())
        _for_rows(n, lambda r: gather(dest_ref[n + r], r, 1).start())

    nxt = jnp.minimum(i + 2, last) * n

    def wait_gathers(s):
        pltpu.make_async_copy(y_hbm.at[pl.ds(0, n)], ybuf_ref.at[s], gsem.at[s]).wait()

    wait_gathers(i % RING)

    def step(cur):
        for r in range(n):
            gather(dest_ref[nxt + r], r, (cur + 2) % RING).start(priority=r % 2)
        o_ref[...] = _rms(h_ref[...] + ybuf_ref[cur], g_ref[...])

    lax.switch(i % RING, [functools.partial(step, cur) for cur in range(RING)])

    @pl.when(i == last)
    def _():
        wait_gathers((i + 1) % RING)
        wait_gathers((i + 2) % RING)


def _final(dest, y_sorted, h1, g):
    T = h1.shape[0]
    assert T // MOVE_TILE >= RING
    row = lambda i, dest: (i, 0)
    grid_spec = pltpu.PrefetchScalarGridSpec(
        num_scalar_prefetch=1,
        grid=(T // MOVE_TILE,),
        in_specs=[pl.BlockSpec(memory_space=pl.ANY),
                  pl.BlockSpec((MOVE_TILE, D_MODEL), row),
                  pl.BlockSpec((1, D_MODEL), lambda i, dest: (0, 0))],
        out_specs=pl.BlockSpec((MOVE_TILE, D_MODEL), row),
        scratch_shapes=[pltpu.VMEM((RING, MOVE_TILE, D_MODEL), F32),
                        pltpu.SemaphoreType.DMA((RING,))],
    )
    return pl.pallas_call(
        _final_kernel,
        grid_spec=grid_spec,
        out_shape=jax.ShapeDtypeStruct((T, D_MODEL), F32),
        compiler_params=_params(("arbitrary",)),
        name="final",
    )(dest, y_sorted, h1, g)


def _block_diag(w):
    h, d, _ = w.shape
    eye = jnp.eye(h, dtype=w.dtype)
    return (eye[:, None, :, None] * w[:, :, None, :]).reshape(h * d, h * d)


def _bias_table(rel_bias):
    rb = rel_bias.astype(F32)
    n_far = PAD_ROWS - MAX_REL + CHUNK
    far = jnp.broadcast_to(rb[:, 2 * MAX_REL:], (ATT_HEADS, n_far))
    near = rb[:, MAX_REL - CHUNK + 1:2 * MAX_REL][:, ::-1]
    diag = jnp.concatenate([far, near], axis=1)
    bias = jnp.stack([diag[:, CHUNK - 1 - q:CHUNK - 1 - q + BAND] for q in range(CHUNK)], axis=1)
    off = jnp.full((ATT_HEADS, CHUNK, CHUNK), -1e30, F32)
    first = jnp.concatenate([bias, off], axis=2)
    second = jnp.concatenate([off, bias], axis=2)
    tiles = jnp.concatenate([first, second], axis=1)
    return tiles.reshape(HEAD_GROUPS, GROUP_HEADS * Q_TILE, WINDOW)


def _dispatch_plan(grp, T):
    onehot = (grp[:, None] == jnp.arange(N_GROUPS)[None, :]).astype(jnp.int32)
    incl = jnp.cumsum(onehot, axis=0)
    counts = incl[-1]
    padded = (counts + EXPERT_BLOCK - 1) // EXPERT_BLOCK * EXPERT_BLOCK
    pend = jnp.cumsum(padded)
    pstart = pend - padded
    dest = jnp.sum((incl - onehot + pstart[None, :]) * onehot, axis=1).astype(jnp.int32)
    n_pad = T + N_GROUPS * EXPERT_BLOCK
    n_blocks = n_pad // EXPERT_BLOCK
    used_end = pstart + counts
    pads = jnp.stack([jnp.concatenate([used_end, pend[-1:]]),
                      jnp.concatenate([pend, jnp.full((1,), n_pad, pend.dtype)])], axis=1)
    pads = pads.reshape(-1).astype(jnp.int32)
    bstart = jnp.arange(n_blocks, dtype=jnp.int32) * EXPERT_BLOCK
    bgroup = jnp.sum((pend[None, :] <= bstart[:, None]).astype(jnp.int32), axis=1)
    bgroup = jnp.minimum(bgroup, N_GROUPS - 1)
    nvalid = (pend[-1:] // EXPERT_BLOCK).astype(jnp.int32)
    return dest, pads, bgroup, nvalid, n_pad


def kernel(x, norm1_g, w_in, conv_w, conv_b, w_rg_a, b_rg_a, w_rg_x, b_rg_x, lru_lambda,
           rel_bias, g_rec_out, g_att_out, w_out, norm2_g, w_group, b_group, w_router,
           b_router, w_e_gate, w_e_up, w_e_down, final_g):
    B, S, D = x.shape
    T = B * S
    assert w_in.shape[0] == 1, "single-layer block"
    (norm1_g, w_in, conv_w, conv_b, w_rg_a, b_rg_a, w_rg_x, b_rg_x, lru_lambda, rel_bias, g_rec_out,
     g_att_out, w_out, norm2_g, w_group, b_group, w_router, b_router, w_e_gate, w_e_up, w_e_down) = (
        a.reshape(a.shape[1:]) for a in (
            norm1_g, w_in, conv_w, conv_b, w_rg_a, b_rg_a, w_rg_x, b_rg_x, lru_lambda, rel_bias,
            g_rec_out, g_att_out, w_out, norm2_g, w_group, b_group, w_router, b_router,
            w_e_gate, w_e_up, w_e_down))
    h = x.reshape(T, D)
    n_exp, _, d_ff = w_e_gate.shape
    u, gate, q, k, v, wd_bf, wo_bf = _inproj(
        h, norm1_g.reshape(1, D), w_in, [w_e_down.reshape(n_exp * d_ff, D), w_out])

    wab = jnp.concatenate([_block_diag(w_rg_a), _block_diag(w_rg_x)], axis=1).astype(BF16)
    bab = jnp.concatenate([b_rg_a.reshape(1, REC_WIDTH), b_rg_x.reshape(1, REC_WIDTH)], axis=1)
    mix_a, wg_bf, wu_bf = _rglru(
        u.reshape(B, S, REC_WIDTH), gate.reshape(B, S, REC_WIDTH),
        conv_w, conv_b.reshape(1, REC_WIDTH), wab, bab,
        lru_lambda.reshape(1, REC_WIDTH), g_rec_out.reshape(1, REC_WIDTH),
        [w_e_gate.reshape(n_exp * D, d_ff), w_e_up.reshape(n_exp * D, d_ff)])

    mix_b = _attention(q.reshape(B, S, ATT_WIDTH), k.reshape(B, S, ATT_WIDTH),
                       v.reshape(B, S, ATT_WIDTH), _bias_table(rel_bias),
                       g_att_out.reshape(1, ATT_WIDTH))

    n_route = N_GROUPS + N_GROUPS * EXPERTS_PER_GROUP
    wr = jnp.concatenate([w_group.astype(F32), w_router.astype(F32)], axis=1)
    wr = jnp.pad(wr, ((0, 0), (0, ROUTE_LANES - n_route)))
    wr_hi = wr.astype(BF16)
    wr_lo = (wr - wr_hi.astype(F32)).astype(BF16)
    br = jnp.concatenate([b_group.astype(F32), b_router.astype(F32)])
    br = jnp.pad(br, (0, ROUTE_LANES - n_route)).reshape(1, ROUTE_LANES)
    h1, xn, route = _outproj(mix_a.reshape(T, REC_WIDTH), mix_b.reshape(T, ATT_WIDTH),
                             wo_bf, h, norm2_g.reshape(1, D),
                             jnp.concatenate([wr_hi, wr_lo], axis=1), br)

    grp = route[EXPERTS_PER_GROUP].astype(jnp.int32)
    dest, pads, bgroup, nvalid, n_pad = _dispatch_plan(grp, T)
    rows_sorted = _dispatch(dest, pads, xn, n_pad)
    y_sorted = _experts(bgroup, nvalid, rows_sorted, wg_bf.reshape(n_exp, D, d_ff),
                        wu_bf.reshape(n_exp, D, d_ff), wd_bf.reshape(n_exp, d_ff, D))
    out = _final(dest, y_sorted, h1, final_g.reshape(1, D))
    return out.reshape(B, S, D)
```
